```python
import jax, jax.numpy as jnp
from jax import lax
import numpy as np

D_MODEL = 1024
BATCH = 4
SEQ = 4096
DEPTH = 1

CHUNK = 64
Q_BLOCK = 128
DSA_HEADS = 8
DSA_HEAD_DIM = D_MODEL // DSA_HEADS
IDX_HEADS = 8
IDX_DIM = 64
IDX_TOPK_MAX = 256
GLA_HEADS = 4
GLA_DK = D_MODEL // 2 // GLA_HEADS
GLA_DV = D_MODEL // GLA_HEADS
GATE_RANK = 16
GATE_TAU = 16.0
N_EXPERTS = 32
TOP_K = 4
D_FF = D_MODEL
SWIGLU_LIMIT = 7.0
SWIGLU_ALPHA = 1.702
EXPERT_BLOCK = 128
EPS = 1e-6

IN_SIZES = (DSA_HEADS * DSA_HEAD_DIM, DSA_HEADS * DSA_HEAD_DIM, DSA_HEADS * DSA_HEAD_DIM,
            IDX_HEADS * IDX_DIM, IDX_DIM, IDX_HEADS,
            GLA_HEADS * GLA_DK, GLA_HEADS * GLA_DK, GLA_HEADS * GLA_DV, GLA_HEADS * GLA_DV,
            GATE_RANK, D_MODEL, D_MODEL)
IN_WIDTH = sum(IN_SIZES)

kernel_name = 'hybrid_dsa_gla_moe_adaln_block'


def _rmsnorm(x, g):
    xf = x.astype(jnp.float32)
    y = xf * lax.rsqrt(jnp.mean(xf * xf, axis=-1, keepdims=True) + EPS)
    return (y * g.astype(jnp.float32)).astype(x.dtype)


def _dsa(q, k, v, qi, ki, wi):
    B, T, H, dh = q.shape
    topk = min(IDX_TOPK_MAX, T // 4)
    nb = T // Q_BLOCK
    kchunk = jnp.arange(T) // CHUNK
    qi = qi * (IDX_DIM ** -0.5)
    wi = wi * (IDX_HEADS ** -0.5)
    gather = jax.vmap(lambda t, i: t[i])

    def blocks(t):
        return t.reshape((B, nb, Q_BLOCK) + t.shape[2:]).swapaxes(0, 1)

    def one_block(args):
        qb, qib, wb, blk = args
        qchunk = (blk * Q_BLOCK + jnp.arange(Q_BLOCK)) // CHUNK
        vis = kchunk[None, :] <= qchunk[:, None]
        rel = jax.nn.relu(jnp.einsum('bqhd,bsd->bqhs', qib, ki).astype(jnp.float32))
        score = jnp.einsum('bqhs,bqh->bqs', rel, wb.astype(jnp.float32))
        score = jnp.where(vis[None], score, -jnp.inf)
        _, sel = lax.top_k(score, topk)
        sel_vis = kchunk[sel] <= qchunk[None, :, None]
        ks = gather(k, sel)
        vs = gather(v, sel)
        logits = jnp.einsum('bqhd,bqkhd->bqhk', qb, ks).astype(jnp.float32) * (dh ** -0.5)
        logits = jnp.where(sel_vis[:, :, None, :], logits, -jnp.inf)
        p = jax.nn.softmax(logits, axis=-1).astype(vs.dtype)
        return jnp.einsum('bqhk,bqkhd->bqhd', p, vs)

    out = lax.map(one_block, (blocks(q), blocks(qi), blocks(wi), jnp.arange(nb)))
    return out.swapaxes(0, 1).reshape(B, T, H * dh)


def _gla(q, k, v, log_a):
    dtype = v.dtype
    B, T, H, DK = q.shape
    DV = v.shape[-1]
    n = T // CHUNK

    def chunks(t):
        return t.astype(jnp.float32).reshape(B, n, CHUNK, H, t.shape[-1]).transpose(1, 0, 3, 2, 4)

    qc = chunks(q) * (DK ** -0.5)
    kc = chunks(k)
    vc = chunks(v)
    G = jnp.cumsum(chunks(log_a), axis=-2)
    causal = jnp.tril(jnp.ones((CHUNK, CHUNK), dtype=bool))

    def step(S, inp):
        qn, kn, vn, gn = inp
        g_last = gn[:, :, -1:, :]
        qe = qn * jnp.exp(gn)
        A = jnp.einsum('bhid,bhjd->bhij', qe, kn * jnp.exp(-gn))
        A = jnp.where(causal, A, 0.0)
        o = jnp.einsum('bhij,bhjv->bhiv', A, vn) + jnp.einsum('bhid,bhdv->bhiv', qe, S)
        S = S * jnp.exp(g_last).swapaxes(-1, -2) + jnp.einsum('bhjd,bhjv->bhdv', kn * jnp.exp(g_last - gn), vn)
        return S, o

    S0 = jnp.zeros((B, H, DK, DV), jnp.float32)
    _, o = lax.scan(step, S0, (qc, kc, vc, G))
    return o.transpose(1, 0, 3, 2, 4).reshape(B, T, H, DV).astype(dtype)


def _moe(h, router_w, router_b, w_gate_up, b_gate_up, w_down, b_down):
    B, T, D = h.shape
    N = B * T
    xt = h.reshape(N, D)
    logits = (xt @ router_w + router_b).astype(jnp.float32)
    top_v, top_e = lax.top_k(logits, TOP_K)
    wts = jax.nn.softmax(top_v, axis=-1)
    n_assign = N * TOP_K
    flat_e = top_e.reshape(-1)
    order = jnp.argsort(flat_e)
    sorted_e = flat_e[order]
    tok = (order // TOP_K).astype(jnp.int32)
    sorted_w = wts.reshape(-1)[order]
    gs = jnp.bincount(flat_e, length=N_EXPERTS).astype(jnp.int32)
    ps = ((gs + EXPERT_BLOCK - 1) // EXPERT_BLOCK) * EXPERT_BLOCK
    off = jnp.cumsum(gs) - gs
    poff = jnp.cumsum(ps) - ps
    pad_pos = poff[sorted_e] + (jnp.arange(n_assign, dtype=jnp.int32) - off[sorted_e])
    n_blocks = -(-n_assign // EXPERT_BLOCK) + N_EXPERTS
    P = n_blocks * EXPERT_BLOCK
    ptok = jnp.full((P,), N, jnp.int32).at[pad_pos].set(tok)
    pw = jnp.zeros((P,), jnp.float32).at[pad_pos].set(sorted_w)
    blk_e = jnp.minimum(jnp.searchsorted(jnp.cumsum(ps), jnp.arange(n_blocks) * EXPERT_BLOCK, side='right'),
                        N_EXPERTS - 1)
    xpad = jnp.concatenate([xt, jnp.zeros((1, D), xt.dtype)], axis=0)
    xs = xpad[ptok].reshape(n_blocks, EXPERT_BLOCK, D)

    def expert_block(args):
        xb, e = args
        hgu = xb @ w_gate_up[e] + b_gate_up[e]
        gate, up = jnp.split(hgu, 2, axis=-1)
        gate = jnp.minimum(gate, SWIGLU_LIMIT)
        up = jnp.clip(up, -SWIGLU_LIMIT, SWIGLU_LIMIT)
        act = (up + 1.0) * (gate * jax.nn.sigmoid(SWIGLU_ALPHA * gate))
        return act @ w_down[e] + b_down[e]

    ys = lax.map(expert_block, (xs, blk_e)).reshape(P, D)
    out = jax.ops.segment_sum(ys * pw[:, None].astype(ys.dtype), ptok, num_segments=N + 1)[:N]
    return out.reshape(B, T, D)


def _layer(x, c, w_ada, b_ada, norm1_g, w_in, w_gate_lr, b_gate, gla_norm_g, w_branch_a, w_branch_b,
           w_out, norm2_g, router_w, router_b, w_gate_up, b_gate_up, w_down, b_down):
    B, T, _ = x.shape
    mod = (jax.nn.silu(c) @ w_ada + b_ada)[:, None, :]
    sh1, sc1, gt1, sh2, sc2, gt2 = jnp.split(mod, 6, axis=-1)
    h = _rmsnorm(x, norm1_g) * (1.0 + sc1) + sh1
    proj = h @ w_in
    split_pts = np.cumsum(IN_SIZES)[:-1].tolist()
    qa, ka, va, qi, ki, wi, qb, kb, vb, gb, lr, ga, gbm = jnp.split(proj, split_pts, axis=-1)

    def heads(t, nh):
        return t.reshape(B, T, nh, -1)

    ya = _dsa(heads(qa, DSA_HEADS), heads(ka, DSA_HEADS), heads(va, DSA_HEADS),
              heads(qi, IDX_HEADS), ki, wi)
    log_a = jax.nn.log_sigmoid((lr @ w_gate_lr + b_gate).astype(jnp.float32)) / GATE_TAU
    ob = _gla(heads(qb, GLA_HEADS), heads(kb, GLA_HEADS), heads(vb, GLA_HEADS), heads(log_a, GLA_HEADS))
    yb = (_rmsnorm(ob, gla_norm_g) * jax.nn.silu(heads(gb, GLA_HEADS))).reshape(B, T, -1)
    merged = jax.nn.sigmoid(ga) * (ya @ w_branch_a) + jax.nn.sigmoid(gbm) * (yb @ w_branch_b)
    x = x + gt1 * (merged @ w_out)
    h2 = _rmsnorm(x, norm2_g) * (1.0 + sc2) + sh2
    x = x + gt2 * _moe(h2, router_w, router_b, w_gate_up, b_gate_up, w_down, b_down)
    return x


def setup_inputs(seed: int = 0) -> dict:
    key = jax.random.key(seed)
    ks = jax.random.split(key, 24)
    D = D_MODEL

    def nrm(k, shape, scale):
        return jax.random.normal(k, shape, jnp.float32) * scale

    return {
        'x': nrm(ks[0], (BATCH, SEQ, D), 1.0),
        'c': nrm(ks[1], (BATCH, D), 1.0),
        'w_ada': nrm(ks[2], (DEPTH, D, 6 * D), D ** -0.5),
        'b_ada': nrm(ks[3], (DEPTH, 6 * D), 0.02),
        'norm1_g': 1.0 + nrm(ks[4], (DEPTH, D), 0.02),
        'w_in': nrm(ks[5], (DEPTH, D, IN_WIDTH), D ** -0.5),
        'w_gate_lr': nrm(ks[6], (DEPTH, GATE_RANK, GLA_HEADS * GLA_DK), GATE_RANK ** -0.5),
        'b_gate': nrm(ks[7], (DEPTH, GLA_HEADS * GLA_DK), 0.1),
        'gla_norm_g': 1.0 + nrm(ks[8], (DEPTH, GLA_DV), 0.02),
        'w_branch_a': nrm(ks[9], (DEPTH, DSA_HEADS * DSA_HEAD_DIM, D), (DSA_HEADS * DSA_HEAD_DIM) ** -0.5),
        'w_branch_b': nrm(ks[10], (DEPTH, GLA_HEADS * GLA_DV, D), (GLA_HEADS * GLA_DV) ** -0.5),
        'w_out': nrm(ks[11], (DEPTH, D, D), D ** -0.5),
        'norm2_g': 1.0 + nrm(ks[12], (DEPTH, D), 0.02),
        'router_w': nrm(ks[13], (DEPTH, D, N_EXPERTS), D ** -0.5),
        'router_b': nrm(ks[14], (DEPTH, N_EXPERTS), 0.01),
        'w_gate_up': nrm(ks[15], (DEPTH, N_EXPERTS, D, 2 * D_FF), D ** -0.5),
        'b_gate_up': nrm(ks[16], (DEPTH, N_EXPERTS, 2 * D_FF), 0.02),
        'w_down': nrm(ks[17], (DEPTH, N_EXPERTS, D_FF, D), D_FF ** -0.5),
        'b_down': nrm(ks[18], (DEPTH, N_EXPERTS, D), 0.02),
        'final_g': 1.0 + nrm(ks[19], (D,), 0.02),
    }


def reference(x, c, w_ada, b_ada, norm1_g, w_in, w_gate_lr, b_gate, gla_norm_g, w_branch_a, w_branch_b,
              w_out, norm2_g, router_w, router_b, w_gate_up, b_gate_up, w_down, b_down, final_g):
    for l in range(DEPTH):
        x = _layer(x, c, w_ada[l], b_ada[l], norm1_g[l], w_in[l], w_gate_lr[l], b_gate[l], gla_norm_g[l],
                   w_branch_a[l], w_branch_b[l], w_out[l], norm2_g[l], router_w[l], router_b[l],
                   w_gate_up[l], b_gate_up[l], w_down[l], b_down[l])
    return _rmsnorm(x, final_g)
```

```python
import functools

import jax
import jax.numpy as jnp
from jax import lax
from jax.experimental import pallas as pl
from jax.experimental.pallas import tpu as pltpu

CHUNK = 64
Q_BLOCK = 128
DSA_HEADS = 8
IDX_HEADS = 8
IDX_DIM = 64
IDX_TOPK_MAX = 256
GLA_HEADS = 4
GATE_RANK = 16
GATE_TAU = 16.0
TOP_K = 4
SWIGLU_LIMIT = 7.0
SWIGLU_ALPHA = 1.702
EPS = 1e-6

LANES = 128
MISC_W = LANES
KI_OFF, WI_OFF, LR_OFF = 0, IDX_DIM, IDX_DIM + IDX_HEADS
NEG_SCORE = -3.0e38
NEG_BIAS = -1.0e30
VMEM_LIMIT = 56 * 1024 * 1024

F32 = jnp.float32
BF16 = jnp.bfloat16
I32 = jnp.int32
HI = lax.Precision.HIGHEST


def _params(n_axes, vmem=VMEM_LIMIT):
    return pltpu.CompilerParams(dimension_semantics=("arbitrary",) * n_axes,
                                vmem_limit_bytes=vmem)


def _log2(n):
    b = n.bit_length() - 1
    assert (1 << b) == n, n
    return b


def _adaln_kernel(c_ref, w_ref, b_ref, o_ref):
    c = c_ref[...]
    s = c * jax.nn.sigmoid(c)
    o_ref[...] = jnp.dot(s, w_ref[...], precision=HI, preferred_element_type=F32) + b_ref[...]


def _adaln(c8, w_ada, b_ada):
    D = c8.shape[1]
    n = w_ada.shape[1] // D
    return pl.pallas_call(
        _adaln_kernel,
        out_shape=jax.ShapeDtypeStruct((c8.shape[0], n * D), F32),
        grid=(n,),
        in_specs=[pl.BlockSpec((c8.shape[0], D), lambda j: (0, 0)),
                  pl.BlockSpec((D, D), lambda j: (0, j)),
                  pl.BlockSpec((1, D), lambda j: (0, j))],
        out_specs=pl.BlockSpec((c8.shape[0], D), lambda j: (0, j)),
        compiler_params=_params(1),
        name="adaln",
    )(c8, w_ada, b_ada.reshape(1, -1))


def _rms_mod(x, g, sc, sh):
    y = x * lax.rsqrt(jnp.mean(x * x, axis=-1, keepdims=True) + EPS)
    return (y * g) * (1.0 + sc) + sh


def _inproj_kernel(x_ref, g_ref, sc_ref, sh_ref, w_ref, wm_ref, *out_refs, widths):
    h = _rms_mod(x_ref[0], g_ref[...], sc_ref[0], sh_ref[0])
    out_refs[-1][0] = jnp.dot(h, wm_ref[...], precision=HI, preferred_element_type=F32)
    hb = h.astype(BF16)
    off = 0
    for o_ref, w in zip(out_refs[:-1], widths):
        o_ref[0] = jnp.dot(hb, w_ref[:, off:off + w], preferred_element_type=F32).astype(BF16)
        off += w


def _inproj(x, g, sc, sh, w_main, w_misc, widths, tm):
    B, T, D = x.shape
    wtot = w_main.shape[1]
    row = lambda b, i: (b, i, 0)
    per_b = lambda b, i: (b, 0, 0)
    const = lambda b, i: (0, 0)
    out_shape = [jax.ShapeDtypeStruct((B, T, w), BF16) for w in widths]
    out_shape.append(jax.ShapeDtypeStruct((B, T, MISC_W), F32))
    out_specs = [pl.BlockSpec((1, tm, w), row) for w in widths]
    out_specs.append(pl.BlockSpec((1, tm, MISC_W), row))
    return pl.pallas_call(
        functools.partial(_inproj_kernel, widths=widths),
        out_shape=out_shape,
        grid=(B, T // tm),
        in_specs=[pl.BlockSpec((1, tm, D), row),
                  pl.BlockSpec((1, D), const),
                  pl.BlockSpec((1, 1, D), per_b),
                  pl.BlockSpec((1, 1, D), per_b),
                  pl.BlockSpec((D, wtot), const, pipeline_mode=pl.Buffered(1)),
                  pl.BlockSpec((D, MISC_W), const, pipeline_mode=pl.Buffered(1))],
        out_specs=out_specs,
        compiler_params=_params(2),
        name="inproj",
    )(x, g, sc, sh, w_main, w_misc)


def _dsa_kernel(q_ref, k_ref, v_ref, qi_ref, mblk_ref, mall_ref, o_ref,
                score_ref, m_ref, l_ref, acc_ref, *, T, KT, topk, dh):
    i = pl.program_id(1)
    nt = ((i + 1) * Q_BLOCK + KT - 1) // KT
    nsub = KT // LANES
    rows = lax.broadcasted_iota(I32, (Q_BLOCK, 1), 0)
    qchunk = (i * Q_BLOCK + rows) >> _log2(CHUNK)
    lim = (qchunk + 1) << _log2(CHUNK)
    col0 = lax.broadcasted_iota(I32, (Q_BLOCK, KT), 1)

    qi = qi_ref[0] * (IDX_DIM ** -0.5)
    wi = mblk_ref[0][:, WI_OFF:WI_OFF + IDX_HEADS] * (IDX_HEADS ** -0.5)

    def score_tile(kt, carry):
        c0 = pl.multiple_of(kt * KT, KT)
        ki_t = mall_ref[0, pl.ds(c0, KT), KI_OFF:KI_OFF + IDX_DIM].astype(BF16)
        s = jnp.zeros((Q_BLOCK, KT), F32)
        for h in range(IDX_HEADS):
            x = lax.dot_general(qi[:, h * IDX_DIM:(h + 1) * IDX_DIM], ki_t,
                                (((1,), (1,)), ((), ())), preferred_element_type=F32)
            s = s + jnp.maximum(x, 0.0) * wi[:, h:h + 1]
        score_ref[:, pl.ds(c0, KT)] = jnp.where(col0 + c0 < lim, s, NEG_SCORE)
        return carry

    lax.fori_loop(0, nt, score_tile, 0)

    def count(pred):
        def body(kt, acc):
            c0 = pl.multiple_of(kt * KT, KT)
            m = jnp.where(pred(score_ref[:, pl.ds(c0, KT)], c0), 1.0, 0.0)
            part = m[:, 0:LANES]
            for j in range(1, nsub):
                part = part + m[:, j * LANES:(j + 1) * LANES]
            return acc + part
        acc = lax.fori_loop(0, nt, body, jnp.zeros((Q_BLOCK, LANES), F32))
        return jnp.sum(acc, axis=1, keepdims=True)

    int_min = jnp.int32(-2 ** 31)

    def key_to_float(u):
        bits = jnp.where(u < 0, u ^ int_min, ~u)
        return lax.bitcast_convert_type(bits, F32)

    def bit_body(it, u):
        trial = u | (jnp.int32(1) << (31 - it))
        t = key_to_float(trial)
        cnt = count(lambda tile, c0: tile >= t)
        return jnp.where(cnt >= topk, trial, u)

    u = lax.fori_loop(0, 32, bit_body, jnp.zeros((Q_BLOCK, 1), I32))
    thr = key_to_float(u)

    c_gt = count(lambda tile, c0: tile > thr)
    c_ge = count(lambda tile, c0: tile >= thr)
    need = topk - c_gt
    excess = jnp.logical_and(c_ge - c_gt > need, thr > NEG_SCORE)

    def tie_search():
        def body(it, v):
            trial = v | (jnp.int32(1) << (_log2(T) - 1 - it))
            cnt = count(lambda tile, c0: jnp.logical_and(tile == thr, col0 + c0 < trial))
            return jnp.where(cnt < need, trial, v)
        v = lax.fori_loop(0, _log2(T), body, jnp.zeros((Q_BLOCK, 1), I32))
        return jnp.where(excess, v, T)

    has_excess = jnp.max(jnp.where(excess, 1.0, 0.0)) > 0.0
    cut = lax.cond(has_excess, tie_search, lambda: jnp.full((Q_BLOCK, 1), T, I32))

    def bias_tile(kt, carry):
        c0 = pl.multiple_of(kt * KT, KT)
        tile = score_ref[:, pl.ds(c0, KT)]
        col = col0 + c0
        sel = jnp.logical_or(tile > thr, jnp.logical_and(tile == thr, col <= cut))
        sel = jnp.logical_and(sel, col < lim)
        score_ref[:, pl.ds(c0, KT)] = jnp.where(sel, 0.0, NEG_BIAS)
        return carry

    lax.fori_loop(0, nt, bias_tile, 0)

    m_ref[...] = jnp.full(m_ref.shape, NEG_BIAS, F32)
    l_ref[...] = jnp.zeros(l_ref.shape, F32)
    acc_ref[...] = jnp.zeros(acc_ref.shape, F32)
    scale = dh ** -0.5

    def att_tile(kt, carry):
        c0 = pl.multiple_of(kt * KT, KT)
        bias = score_ref[:, pl.ds(c0, KT)]
        for h in range(DSA_HEADS):
            hs = slice(h * dh, (h + 1) * dh)
            lg = lax.dot_general(q_ref[0, :, hs], k_ref[0, pl.ds(c0, KT), hs],
                                 (((1,), (1,)), ((), ())), preferred_element_type=F32)
            lg = lg * scale + bias
            m_old = m_ref[h]
            m_new = jnp.maximum(m_old, jnp.max(lg, axis=1, keepdims=True))
            alpha = jnp.exp(m_old - m_new)
            p = jnp.exp(lg - m_new)
            l_ref[h] = alpha * l_ref[h] + jnp.sum(p, axis=1, keepdims=True)
            pv = jnp.dot(p.astype(BF16), v_ref[0, pl.ds(c0, KT), hs], preferred_element_type=F32)
            acc_ref[:, hs] = alpha * acc_ref[:, hs] + pv
            m_ref[h] = m_new
        return carry

    lax.fori_loop(0, nt, att_tile, 0)
    for h in range(DSA_HEADS):
        hs = slice(h * dh, (h + 1) * dh)
        o_ref[0, :, hs] = (acc_ref[:, hs] / l_ref[h]).astype(BF16)


def _dsa(qa, ka, va, qi, misc):
    B, T, D = qa.shape
    dh = D // DSA_HEADS
    KT = min(512, T)
    topk = min(IDX_TOPK_MAX, T // 4)
    blk = lambda b, i: (b, i, 0)
    full = lambda b, i: (b, 0, 0)
    return pl.pallas_call(
        functools.partial(_dsa_kernel, T=T, KT=KT, topk=topk, dh=dh),
        out_shape=jax.ShapeDtypeStruct((B, T, D), BF16),
        grid=(B, T // Q_BLOCK),
        in_specs=[pl.BlockSpec((1, Q_BLOCK, D), blk),
                  pl.BlockSpec((1, T, D), full),
                  pl.BlockSpec((1, T, D), full),
                  pl.BlockSpec((1, Q_BLOCK, IDX_HEADS * IDX_DIM), blk),
                  pl.BlockSpec((1, Q_BLOCK, MISC_W), blk),
                  pl.BlockSpec((1, T, MISC_W), full)],
        out_specs=pl.BlockSpec((1, Q_BLOCK, D), blk),
        scratch_shapes=[pltpu.VMEM((Q_BLOCK, T), F32),
                        pltpu.VMEM((DSA_HEADS, Q_BLOCK, 1), F32),
                        pltpu.VMEM((DSA_HEADS, Q_BLOCK, 1), F32),
                        pltpu.VMEM((Q_BLOCK, D), F32)],
        compiler_params=_params(2),
        name="dsa",
    )(qa, ka, va, qi, misc, misc)


def _gla_kernel(q_ref, k_ref, v_ref, gb_ref, misc_ref, wg_ref, bg_ref, gn_ref, o_ref, st_ref,
                *, tb, dk, dv):
    @pl.when(pl.program_id(1) == 0)
    def _():
        st_ref[...] = jnp.zeros(st_ref.shape, F32)

    x = jnp.dot(misc_ref[0], wg_ref[...], precision=HI, preferred_element_type=F32) + bg_ref[...]
    log_a = (jnp.minimum(x, 0.0) - jnp.log1p(jnp.exp(-jnp.abs(x)))) / GATE_TAU

    r = lax.broadcasted_iota(I32, (CHUNK, CHUNK), 0)
    c = lax.broadcasted_iota(I32, (CHUNK, CHUNK), 1)
    causal = r >= c
    tril = jnp.where(causal, 1.0, 0.0)
    gn = gn_ref[...]
    contract_last = (((1,), (1,)), ((), ()))
    contract_first = (((0,), (0,)), ((), ()))

    for ci in range(tb // CHUNK):
        rs = slice(ci * CHUNK, (ci + 1) * CHUNK)
        g_all = jnp.dot(tril, log_a[rs], precision=HI, preferred_element_type=F32)
        for h in range(GLA_HEADS):
            ks = slice(h * dk, (h + 1) * dk)
            vs = slice(h * dv, (h + 1) * dv)
            g = g_all[:, ks]
            g_last = g[CHUNK - 1:CHUNK, :]
            qe = (q_ref[0, rs, ks].astype(F32) * (dk ** -0.5)) * jnp.exp(g)
            kh = k_ref[0, rs, ks].astype(F32)
            vh = v_ref[0, rs, vs]
            a = lax.dot_general(qe.astype(BF16), (kh * jnp.exp(-g)).astype(BF16), contract_last,
                                preferred_element_type=F32)
            a = jnp.where(causal, a, 0.0)
            st = st_ref[h]
            o = (jnp.dot(a.astype(BF16), vh, preferred_element_type=F32)
                 + lax.dot_general(qe.astype(BF16), st.astype(BF16), contract_last,
                                   preferred_element_type=F32))
            upd = lax.dot_general(vh, (kh * jnp.exp(g_last - g)).astype(BF16), contract_first,
                                  preferred_element_type=F32)
            st_ref[h] = st * jnp.exp(g_last) + upd
            y = o * lax.rsqrt(jnp.mean(o * o, axis=-1, keepdims=True) + EPS) * gn
            gate = gb_ref[0, rs, vs].astype(F32)
            o_ref[0, rs, vs] = (y * (gate * jax.nn.sigmoid(gate))).astype(BF16)


def _gla(qb, kb, vb, gb, misc, wg_pad, b_gate, gla_norm_g, tb):
    B, T, HK = qb.shape
    HV = vb.shape[2]
    dk, dv = HK // GLA_HEADS, HV // GLA_HEADS
    blk = lambda b, i: (b, i, 0)
    const = lambda b, i: (0, 0)
    return pl.pallas_call(
        functools.partial(_gla_kernel, tb=tb, dk=dk, dv=dv),
        out_shape=jax.ShapeDtypeStruct((B, T, HV), BF16),
        grid=(B, T // tb),
        in_specs=[pl.BlockSpec((1, tb, HK), blk),
                  pl.BlockSpec((1, tb, HK), blk),
                  pl.BlockSpec((1, tb, HV), blk),
                  pl.BlockSpec((1, tb, HV), blk),
                  pl.BlockSpec((1, tb, MISC_W), blk),
                  pl.BlockSpec((MISC_W, HK), const),
                  pl.BlockSpec((1, HK), const),
                  pl.BlockSpec((1, dv), const)],
        out_specs=pl.BlockSpec((1, tb, HV), blk),
        scratch_shapes=[pltpu.VMEM((GLA_HEADS, dv, dk), F32)],
        compiler_params=_params(2),
        name="gla",
    )(qb, kb, vb, gb, misc, wg_pad, b_gate.reshape(1, -1), gla_norm_g.reshape(1, -1))


def _merge_kernel(ya_ref, yb_ref, ga_ref, gbm_ref, x_ref, gt_ref, sc_ref, sh_ref, g2_ref,
                  wa_ref, wb_ref, wo_ref, rw_ref, rb_ref,
                  x1_ref, h2_ref, e_ref, w_ref, cnt_ref, *, n_exp):
    pa = jnp.dot(ya_ref[0], wa_ref[...], preferred_element_type=F32)
    pb = jnp.dot(yb_ref[0], wb_ref[...], preferred_element_type=F32)
    merged = (jax.nn.sigmoid(ga_ref[0].astype(F32)) * pa
              + jax.nn.sigmoid(gbm_ref[0].astype(F32)) * pb)
    mo = jnp.dot(merged.astype(BF16), wo_ref[...], preferred_element_type=F32)
    x1 = x_ref[0] + gt_ref[0] * mo
    x1_ref[0] = x1
    h2 = _rms_mod(x1, g2_ref[...], sc_ref[0], sh_ref[0])
    h2_ref[0] = h2

    logits = jnp.dot(h2, rw_ref[...], precision=HI, preferred_element_type=F32) + rb_ref[...]
    tm = logits.shape[0]
    lane = lax.broadcasted_iota(I32, (tm, LANES), 1)
    logits = jnp.where(lane < n_exp, logits, -jnp.inf)
    top_v, top_e = [], []
    for _ in range(TOP_K):
        mx = jnp.max(logits, axis=1, keepdims=True)
        idx = jnp.min(jnp.where(logits == mx, lane, LANES), axis=1, keepdims=True)
        logits = jnp.where(lane == idx, -jnp.inf, logits)
        top_v.append(mx)
        top_e.append(idx)
    ex = [jnp.exp(v - top_v[0]) for v in top_v]
    den = ex[0]
    for t in ex[1:]:
        den = den + t
    e_out = jnp.zeros((tm, LANES), I32)
    w_out = jnp.zeros((tm, LANES), F32)
    hot = jnp.zeros((tm, LANES), F32)
    for k in range(TOP_K):
        e_out = jnp.where(lane == k, top_e[k], e_out)
        w_out = jnp.where(lane == k, ex[k] / den, w_out)
        hot = hot + jnp.where(lane == top_e[k], 1.0, 0.0)
    e_ref[0] = e_out
    w_ref[0] = w_out

    @pl.when(jnp.logical_and(pl.program_id(0) == 0, pl.program_id(1) == 0))
    def _():
        cnt_ref[...] = jnp.zeros(cnt_ref.shape, F32)

    cnt_ref[0:1, :] += jnp.sum(hot, axis=0, keepdims=True)


def _merge(ya, yb, ga, gbm, x, gt1, sc2, sh2, g2, wa, wb, wo, rw_pad, rb_pad, n_exp, tm):
    B, T, D = x.shape
    row = lambda b, i: (b, i, 0)
    per_b = lambda b, i: (b, 0, 0)
    const = lambda b, i: (0, 0)
    resident = functools.partial(pl.BlockSpec, index_map=const, pipeline_mode=pl.Buffered(1))
    return pl.pallas_call(
        functools.partial(_merge_kernel, n_exp=n_exp),
        out_shape=[jax.ShapeDtypeStruct((B, T, D), F32),
                   jax.ShapeDtypeStruct((B, T, D), F32),
                   jax.ShapeDtypeStruct((B, T, LANES), I32),
                   jax.ShapeDtypeStruct((B, T, LANES), F32),
                   jax.ShapeDtypeStruct((8, LANES), F32)],
        grid=(B, T // tm),
        in_specs=[pl.BlockSpec((1, tm, D), row),
                  pl.BlockSpec((1, tm, D), row),
                  pl.BlockSpec((1, tm, D), row),
                  pl.BlockSpec((1, tm, D), row),
                  pl.BlockSpec((1, tm, D), row),
                  pl.BlockSpec((1, 1, D), per_b),
                  pl.BlockSpec((1, 1, D), per_b),
                  pl.BlockSpec((1, 1, D), per_b),
                  pl.BlockSpec((1, D), const),
                  resident((D, D)), resident((D, D)), resident((D, D)),
                  resident((D, LANES)), pl.BlockSpec((1, LANES), const)],
        out_specs=[pl.BlockSpec((1, tm, D), row),
                   pl.BlockSpec((1, tm, D), row),
                   pl.BlockSpec((1, tm, LANES), row),
                   pl.BlockSpec((1, tm, LANES), row),
                   pl.BlockSpec((8, LANES), const)],
        compiler_params=_params(2),
        name="merge",
    )(ya, yb, ga, gbm, x, gt1, sc2, sh2, g2, wa, wb, wo, rw_pad, rb_pad)


def _plan_kernel(e_ref, cnt_ref, slot_ref, blk_ref, nblk_ref, base_ref, *, n_exp, tmb, nbp):
    tt = e_ref.shape[0]
    lane1 = lax.broadcasted_iota(I32, (1, LANES), 1)

    @pl.when(pl.program_id(0) == 0)
    def _():
        cnt = cnt_ref[0:1, :]
        padded = jnp.where(lane1 < n_exp, jnp.ceil(cnt / tmb) * tmb, 0.0)
        r = lax.broadcasted_iota(I32, (LANES, LANES), 0)
        c = lax.broadcasted_iota(I32, (LANES, LANES), 1)
        upper = jnp.where(r < c, 1.0, 0.0)
        start = jnp.dot(jnp.broadcast_to(padded, (8, LANES)), upper, precision=HI,
                        preferred_element_type=F32)[0:1, :]
        base_ref[...] = start
        end = start + padded
        jrow = lax.broadcasted_iota(I32, (nbp, LANES), 0).astype(F32) * tmb
        owner = jnp.sum(jnp.where(jnp.logical_and(end <= jrow, lane1 < n_exp), 1.0, 0.0),
                        axis=1, keepdims=True)
        blk_ref[...] = jnp.broadcast_to(jnp.minimum(owner, n_exp - 1.0), (nbp, LANES)).astype(I32)
        total = jnp.sum(padded, axis=1, keepdims=True)
        nblk_ref[...] = jnp.broadcast_to(total / tmb, (8, LANES)).astype(I32)

    e = e_ref[...]
    lane = lax.broadcasted_iota(I32, (tt, LANES), 1)
    hot = jnp.zeros((tt, LANES), F32)
    for k in range(TOP_K):
        hot = hot + jnp.where(lane == e[:, k:k + 1], 1.0, 0.0)
    r = lax.broadcasted_iota(I32, (tt, tt), 0)
    c = lax.broadcasted_iota(I32, (tt, tt), 1)
    lower = jnp.where(r > c, 1.0, 0.0).astype(BF16)
    rank = jnp.dot(lower, hot.astype(BF16), preferred_element_type=F32)
    pos = base_ref[...] + rank
    out = jnp.zeros((tt, LANES), F32)
    for k in range(TOP_K):
        sk = jnp.sum(jnp.where(lane == e[:, k:k + 1], pos, 0.0), axis=1, keepdims=True)
        out = jnp.where(lane == k, sk, out)
    slot_ref[...] = out.astype(I32)
    base_ref[...] += jnp.sum(hot, axis=0, keepdims=True)


def _plan(e_pad, cnt, n_exp, tmb, nbp, tt):
    N = e_pad.shape[0]
    return pl.pallas_call(
        functools.partial(_plan_kernel, n_exp=n_exp, tmb=tmb, nbp=nbp),
        out_shape=[jax.ShapeDtypeStruct((N, LANES), I32),
                   jax.ShapeDtypeStruct((nbp, LANES), I32),
                   jax.ShapeDtypeStruct((8, LANES), I32)],
        grid=(N // tt,),
        in_specs=[pl.BlockSpec((tt, LANES), lambda i: (i, 0)),
                  pl.BlockSpec((8, LANES), lambda i: (0, 0))],
        out_specs=[pl.BlockSpec((tt, LANES), lambda i: (i, 0)),
                   pl.BlockSpec((nbp, LANES), lambda i: (0, 0)),
                   pl.BlockSpec((8, LANES), lambda i: (0, 0))],
        scratch_shapes=[pltpu.VMEM((1, LANES), F32)],
        compiler_params=_params(1),
        name="moe_plan",
    )(e_pad, cnt)


def _dispatch_kernel(slot_ref, h_ref, xs_in_ref, xs_ref, sem, *, tt):
    del xs_in_ref

    def row_copy(t, k):
        s = slot_ref[t * TOP_K + k]
        return pltpu.make_async_copy(h_ref.at[pl.ds(t, 1)], xs_ref.at[pl.ds(s, 1)], sem)

    def start(t, carry):
        for k in range(TOP_K):
            row_copy(t, k).start()
        return carry

    def wait(t, carry):
        for k in range(TOP_K):
            row_copy(t, k).wait()
        return carry

    lax.fori_loop(0, tt, start, 0)
    lax.fori_loop(0, tt, wait, 0)


def _dispatch(slots_flat, h2, xs0, tt):
    N, D = h2.shape
    return pl.pallas_call(
        functools.partial(_dispatch_kernel, tt=tt),
        out_shape=jax.ShapeDtypeStruct(xs0.shape, xs0.dtype),
        grid=(N // tt,),
        in_specs=[pl.BlockSpec((tt * TOP_K,), lambda i: (i,), memory_space=pltpu.SMEM),
                  pl.BlockSpec((tt, D), lambda i: (i, 0)),
                  pl.BlockSpec(memory_space=pl.ANY)],
        out_specs=pl.BlockSpec(memory_space=pl.ANY),
        scratch_shapes=[pltpu.SemaphoreType.DMA],
        input_output_aliases={2: 0},
        compiler_params=_params(1),
        name="moe_dispatch",
    )(slots_flat, h2, xs0)


def _experts_kernel(blk_ref, nblk_ref, xs_ref, wgu_ref, bgu_ref, wd_ref, bd_ref, ys_ref,
                    wgu_bf, wd_bf, *, ff):
    j = pl.program_id(0)
    e = blk_ref[j]
    fresh = jnp.logical_or(j == 0, e != blk_ref[jnp.maximum(j - 1, 0)])
    active = j < nblk_ref[0]

    @pl.when(jnp.logical_and(active, fresh))
    def _():
        wgu_bf[...] = wgu_ref[0].astype(BF16)
        wd_bf[...] = wd_ref[0].astype(BF16)

    @pl.when(active)
    def _():
        hgu = jnp.dot(xs_ref[...].astype(BF16), wgu_bf[...], preferred_element_type=F32) + bgu_ref[0]
        gate = jnp.minimum(hgu[:, :ff], SWIGLU_LIMIT)
        up = jnp.clip(hgu[:, ff:], -SWIGLU_LIMIT, SWIGLU_LIMIT)
        act = (up + 1.0) * (gate * jax.nn.sigmoid(SWIGLU_ALPHA * gate))
        ys_ref[...] = jnp.dot(act.astype(BF16), wd_bf[...], preferred_element_type=F32) + bd_ref[0]

    @pl.when(jnp.logical_not(active))
    def _():
        ys_ref[...] = jnp.zeros(ys_ref.shape, F32)


def _experts(blk_e, nblk, xs, w_gate_up, b_gate_up, w_down, b_down, tmb):
    P, D = xs.shape
    E, _, ff2 = w_gate_up.shape
    ff = ff2 // 2
    by_expert = lambda j, blk, nb: (blk[j], 0, 0)
    return pl.pallas_call(
        functools.partial(_experts_kernel, ff=ff),
        out_shape=jax.ShapeDtypeStruct((P, D), F32),
        grid_spec=pltpu.PrefetchScalarGridSpec(
            num_scalar_prefetch=2,
            grid=(P // tmb,),
            in_specs=[pl.BlockSpec((tmb, D), lambda j, blk, nb: (j, 0)),
                      pl.BlockSpec((1, D, ff2), by_expert),
                      pl.BlockSpec((1, 1, ff2), by_expert),
                      pl.BlockSpec((1, ff, D), by_expert),
                      pl.BlockSpec((1, 1, D), by_expert)],
            out_specs=pl.BlockSpec((tmb, D), lambda j, blk, nb: (j, 0)),
            scratch_shapes=[pltpu.VMEM((D, ff2), BF16), pltpu.VMEM((ff, D), BF16)]),
        compiler_params=_params(1),
        name="moe_experts",
    )(blk_e, nblk, xs, w_gate_up, b_gate_up.reshape(E, 1, ff2), w_down, b_down.reshape(E, 1, D))


def _combine_kernel(slot_ref, ys_ref, w_ref, x1_ref, gt_ref, fg_ref, o_ref, buf, sem, *, tt):
    def row_copy(t, k):
        s = slot_ref[t * TOP_K + k]
        return pltpu.make_async_copy(ys_ref.at[pl.ds(s, 1)], buf.at[k, pl.ds(t, 1)], sem)

    def start(t, carry):
        for k in range(TOP_K):
            row_copy(t, k).start()
        return carry

    def wait(t, carry):
        for k in range(TOP_K):
            row_copy(t, k).wait()
        return carry

    lax.fori_loop(0, tt, start, 0)
    lax.fori_loop(0, tt, wait, 0)

    w = w_ref[0]
    moe = buf[0] * w[:, 0:1]
    for k in range(1, TOP_K):
        moe = moe + buf[k] * w[:, k:k + 1]
    x2 = x1_ref[0] + gt_ref[0] * moe
    y = x2 * lax.rsqrt(jnp.mean(x2 * x2, axis=-1, keepdims=True) + EPS)
    o_ref[0] = y * fg_ref[...]


def _combine(slots_flat, ys, w_pad, x1, gt2, final_g, tt):
    B, T, D = x1.shape
    nt = T // tt
    row = lambda b, i: (b, i, 0)
    return pl.pallas_call(
        functools.partial(_combine_kernel, tt=tt),
        out_shape=jax.ShapeDtypeStruct((B, T, D), F32),
        grid=(B, nt),
        in_specs=[pl.BlockSpec((tt * TOP_K,), lambda b, i: (b * nt + i,), memory_space=pltpu.SMEM),
                  pl.BlockSpec(memory_space=pl.ANY),
                  pl.BlockSpec((1, tt, LANES), row),
                  pl.BlockSpec((1, tt, D), row),
                  pl.BlockSpec((1, 1, D), lambda b, i: (b, 0, 0)),
                  pl.BlockSpec((1, D), lambda b, i: (0, 0))],
        out_specs=pl.BlockSpec((1, tt, D), row),
        scratch_shapes=[pltpu.VMEM((TOP_K, tt, D), F32), pltpu.SemaphoreType.DMA],
        compiler_params=_params(2),
        name="moe_combine",
    )(slots_flat, ys, w_pad, x1, gt2, final_g.reshape(1, -1))


def _layer(x, c, w_ada, b_ada, norm1_g, w_in, w_gate_lr, b_gate, gla_norm_g, w_branch_a,
           w_branch_b, w_out, norm2_g, router_w, router_b, w_gate_up, b_gate_up, w_down, b_down,
           final_g):
    B, T, D = x.shape
    N = B * T
    n_exp = router_w.shape[1]
    hk = w_gate_lr.shape[1]
    hv = w_branch_b.shape[0]
    ni = IDX_HEADS * IDX_DIM

    c8 = jnp.zeros((8, D), F32).at[:B].set(c)
    mod = _adaln(c8, w_ada, b_ada)[:B]
    sh1, sc1, gt1, sh2, sc2, gt2 = [m.reshape(B, 1, D) for m in jnp.split(mod, 6, axis=-1)]

    sizes = (D, D, D, ni, IDX_DIM, IDX_HEADS, hk, hk, hv, hv, GATE_RANK, D, D)
    offs = [0]
    for s in sizes:
        offs.append(offs[-1] + s)
    seg = lambda j: w_in[:, offs[j]:offs[j + 1]]
    order = (0, 1, 2, 3, 6, 7, 8, 9, 11, 12)
    widths = tuple(sizes[j] for j in order)
    w_main = jnp.concatenate([seg(j) for j in order], axis=1).astype(BF16)
    w_misc = jnp.concatenate(
        [seg(4), seg(5), seg(10), jnp.zeros((D, MISC_W - IDX_DIM - IDX_HEADS - GATE_RANK), F32)],
        axis=1)
    qa, ka, va, qi, qb, kb, vb, gb, ga, gbm, misc = _inproj(
        x, norm1_g.reshape(1, D), sc1, sh1, w_main, w_misc, widths, min(256, T))

    ya = _dsa(qa, ka, va, qi, misc)

    wg_pad = jnp.zeros((MISC_W, hk), F32).at[LR_OFF:LR_OFF + GATE_RANK].set(w_gate_lr)
    yb = _gla(qb, kb, vb, gb, misc, wg_pad, b_gate, gla_norm_g, min(256, T))

    rw_pad = jnp.zeros((D, LANES), F32).at[:, :n_exp].set(router_w)
    rb_pad = jnp.zeros((1, LANES), F32).at[0, :n_exp].set(router_b)
    x1, h2, e_pad, w_pad, cnt = _merge(
        ya, yb, ga, gbm, x, gt1, sc2, sh2, norm2_g.reshape(1, D),
        w_branch_a.astype(BF16), w_branch_b.astype(BF16), w_out.astype(BF16),
        rw_pad, rb_pad, n_exp, min(512, T))

    tmb = 256
    nb = -(-N * TOP_K // tmb) + n_exp
    nbp = -(-nb // 8) * 8
    slots, blk_e, nblk = _plan(e_pad.reshape(N, LANES), cnt, n_exp, tmb, nbp, min(512, N))

    tt_d = min(256, T)
    slots_flat = slots[:, :TOP_K].reshape(-1)
    xs = _dispatch(slots_flat, h2.reshape(N, D), jnp.zeros((nb * tmb, D), F32), tt_d)
    ys = _experts(blk_e[:nb, 0], nblk[0, :1], xs, w_gate_up, b_gate_up, w_down, b_down, tmb)
    tt_c = min(128, T)
    return _combine(slots_flat, ys, w_pad, x1, gt2, final_g, tt_c)


def kernel(x, c, w_ada, b_ada, norm1_g, w_in, w_gate_lr, b_gate, gla_norm_g, w_branch_a, w_branch_b,
           w_out, norm2_g, router_w, router_b, w_gate_up, b_gate_up, w_down, b_down, final_g):
    assert w_ada.shape[0] == 1, "single-layer block"
    return _layer(x, c, w_ada[0], b_ada[0], norm1_g[0], w_in[0], w_gate_lr[0], b_gate[0],
                  gla_norm_g[0], w_branch_a[0], w_branch_b[0], w_out[0], norm2_g[0], router_w[0],
                  router_b[0], w_gate_up[0], b_gate_up[0], w_down[0], b_down[0], final_g)
```

```python
import functools

import jax
import jax.numpy as jnp
from jax import lax
from jax.experimental import pallas as pl
from jax.experimental.pallas import tpu as pltpu

CHUNK = 64
Q_BLOCK = 128
DSA_HEADS = 8
IDX_HEADS = 8
IDX_DIM = 64
IDX_TOPK_MAX = 256
GLA_HEADS = 4
GATE_RANK = 16
GATE_TAU = 16.0
TOP_K = 4
SWIGLU_LIMIT = 7.0
SWIGLU_ALPHA = 1.702
EPS = 1e-6

LANES = 128
MISC_W = LANES
KI_OFF, WI_OFF, LR_OFF = 0, IDX_DIM, IDX_DIM + IDX_HEADS
NEG_SCORE = -3.0e38
NEG_BIAS = -1.0e30
VMEM_LIMIT = 56 * 1024 * 1024

F32 = jnp.float32
BF16 = jnp.bfloat16
I32 = jnp.int32
HI = lax.Precision.HIGHEST


def _params(n_axes, vmem=VMEM_LIMIT):
    return pltpu.CompilerParams(dimension_semantics=("arbitrary",) * n_axes,
                                vmem_limit_bytes=vmem)


def _log2(n):
    b = n.bit_length() - 1
    assert (1 << b) == n, n
    return b


def _adaln_kernel(c_ref, w_ref, b_ref, o_ref):
    c = c_ref[...]
    s = c * jax.nn.sigmoid(c)
    o_ref[...] = jnp.dot(s, w_ref[...], precision=HI, preferred_element_type=F32) + b_ref[...]


def _adaln(c8, w_ada, b_ada):
    D = c8.shape[1]
    n = w_ada.shape[1] // D
    return pl.pallas_call(
        _adaln_kernel,
        out_shape=jax.ShapeDtypeStruct((c8.shape[0], n * D), F32),
        grid=(n,),
        in_specs=[pl.BlockSpec((c8.shape[0], D), lambda j: (0, 0)),
                  pl.BlockSpec((D, D), lambda j: (0, j)),
                  pl.BlockSpec((1, D), lambda j: (0, j))],
        out_specs=pl.BlockSpec((c8.shape[0], D), lambda j: (0, j)),
        compiler_params=_params(1),
        name="adaln",
    )(c8, w_ada, b_ada.reshape(1, -1))


def _rms_mod(x, g, sc, sh):
    y = x * lax.rsqrt(jnp.mean(x * x, axis=-1, keepdims=True) + EPS)
    return (y * g) * (1.0 + sc) + sh


_CONTRACT_LAST = (((1,), (1,)), ((), ()))
_CONTRACT_FIRST = (((0,), (0,)), ((), ()))


def _inproj_kernel(x_ref, g_ref, sc_ref, sh_ref, w_ref, wvt_ref, wm_ref, wmt_ref, *out_refs, widths):
    h = _rms_mod(x_ref[0], g_ref[...], sc_ref[0], sh_ref[0])
    hb = h.astype(BF16)
    out_refs[-1][0] = lax.dot_general(wmt_ref[...], h, _CONTRACT_LAST, precision=HI,
                                      preferred_element_type=F32)
    out_refs[-2][0] = jnp.dot(h, wm_ref[...], precision=HI, preferred_element_type=F32)
    out_refs[-3][0, 0] = lax.dot_general(wvt_ref[...], hb, _CONTRACT_LAST,
                                         preferred_element_type=F32).astype(BF16)
    off = 0
    for o_ref, w in zip(out_refs[:-3], widths):
        o_ref[0] = jnp.dot(hb, w_ref[:, off:off + w], preferred_element_type=F32).astype(BF16)
        off += w


def _inproj(x, g, sc, sh, w_main, w_vt, w_misc, widths, tm):
    B, T, D = x.shape
    wtot = w_main.shape[1]
    dv = w_vt.shape[0]
    row = lambda b, i: (b, i, 0)
    per_b = lambda b, i: (b, 0, 0)
    const = lambda b, i: (0, 0)
    resident = functools.partial(pl.BlockSpec, index_map=const, pipeline_mode=pl.Buffered(1))
    out_shape = [jax.ShapeDtypeStruct((B, T, w), BF16) for w in widths]
    out_specs = [pl.BlockSpec((1, tm, w), row) for w in widths]
    out_shape += [jax.ShapeDtypeStruct((B, T // tm, dv, tm), BF16),
                  jax.ShapeDtypeStruct((B, T, MISC_W), F32),
                  jax.ShapeDtypeStruct((B, MISC_W, T), F32)]
    out_specs += [pl.BlockSpec((1, 1, dv, tm), lambda b, i: (b, i, 0, 0)),
                  pl.BlockSpec((1, tm, MISC_W), row),
                  pl.BlockSpec((1, MISC_W, tm), lambda b, i: (b, 0, i))]
    return pl.pallas_call(
        functools.partial(_inproj_kernel, widths=widths),
        out_shape=out_shape,
        grid=(B, T // tm),
        in_specs=[pl.BlockSpec((1, tm, D), row),
                  pl.BlockSpec((1, D), const),
                  pl.BlockSpec((1, 1, D), per_b),
                  pl.BlockSpec((1, 1, D), per_b),
                  resident((D, wtot)), resident((dv, D)),
                  resident((D, MISC_W)), resident((MISC_W, D))],
        out_specs=out_specs,
        compiler_params=_params(2),
        name="inproj",
    )(x, g, sc, sh, w_main, w_vt, w_misc, w_misc.T)


def _pair_loop(n, body, init):
    def two(j, carry):
        return body(2 * j + 1, body(2 * j, carry))
    carry = lax.fori_loop(0, n >> 1, two, init)
    return lax.cond((n & 1) == 1, lambda c: body(n - 1, c), lambda c: c, carry)


def _dsa_kernel(q_ref, k_ref, vt_ref, qi_ref, mall_ref, mt_ref, o_ref, sc_ref, lg_ref, acc_ref,
                *, T, DQ, KT, KC, topk, dh):
    i = pl.program_id(1)
    nt = ((i + 1) * DQ + KT - 1) // KT
    qpos = i * DQ + lax.broadcasted_iota(I32, (1, DQ), 1)
    lim = ((qpos >> _log2(CHUNK)) + 1) << _log2(CHUNK)
    krow = lax.broadcasted_iota(I32, (KT, DQ), 0)

    def tile(kt):
        return pl.ds(pl.multiple_of(kt * KT, KT), KT)

    wi_t = mt_ref[0, WI_OFF:WI_OFF + IDX_HEADS, :] * ((IDX_HEADS ** -0.5) * (IDX_DIM ** -0.5))

    def score_tile(kt, carry):
        ki_t = mall_ref[0, tile(kt), :].astype(BF16)
        s = jnp.zeros((KT, DQ), F32)
        for h in range(IDX_HEADS):
            x = lax.dot_general(ki_t, qi_ref[0, :, h * LANES:(h + 1) * LANES], _CONTRACT_LAST,
                                preferred_element_type=F32)
            s = s + jnp.maximum(x, 0.0) * wi_t[h:h + 1, :]
        sc_ref[tile(kt), :] = jnp.where(krow + kt * KT < lim, s, NEG_SCORE)
        return carry

    lax.fori_loop(0, nt, score_tile, 0)

    def count(pred):
        def body(kt, acc):
            m = jnp.where(pred(sc_ref[tile(kt), :], kt * KT), 1.0, 0.0)
            return acc + jnp.sum(m, axis=0, keepdims=True)
        return lax.fori_loop(0, nt, body, jnp.zeros((1, DQ), F32))

    int_min = jnp.int32(-2 ** 31)

    def key_to_float(u):
        bits = jnp.where(u < 0, u ^ int_min, ~u)
        return lax.bitcast_convert_type(bits, F32)

    def bit_body(it, u):
        trial = u | (jnp.int32(1) << (31 - it))
        t = key_to_float(trial)
        cnt = count(lambda s, c0: s >= t)
        return jnp.where(cnt >= topk, trial, u)

    u = lax.fori_loop(0, 32, bit_body, jnp.zeros((1, DQ), I32))
    thr = key_to_float(u)

    c_gt = count(lambda s, c0: s > thr)
    c_ge = count(lambda s, c0: s >= thr)
    need = topk - c_gt
    excess = jnp.logical_and(c_ge - c_gt > need, thr > NEG_SCORE)

    def tie_search():
        def body(it, v):
            trial = v | (jnp.int32(1) << (_log2(T) - 1 - it))
            cnt = count(lambda s, c0: jnp.logical_and(s == thr, krow + c0 < trial))
            return jnp.where(cnt < need, trial, v)
        v = lax.fori_loop(0, _log2(T), body, jnp.zeros((1, DQ), I32))
        return jnp.where(excess, v, T)

    has_excess = jnp.max(jnp.where(excess, 1.0, 0.0)) > 0.0
    cut = lax.cond(has_excess, tie_search, lambda: jnp.full((1, DQ), T, I32))

    def bias_tile(kt, carry):
        s = sc_ref[tile(kt), :]
        key = krow + kt * KT
        sel = jnp.logical_or(s > thr, jnp.logical_and(s == thr, key <= cut))
        sel = jnp.logical_and(sel, key < lim)
        sc_ref[tile(kt), :] = jnp.where(sel, 0.0, NEG_BIAS)
        return carry

    lax.fori_loop(0, nt, bias_tile, 0)

    c_exp = (dh ** -0.5) * 1.4426950408889634

    for h in range(DSA_HEADS):
        hs = slice(h * dh, (h + 1) * dh)

        def pass_a(kt, mx):
            lg = lax.dot_general(k_ref[0, tile(kt), hs], q_ref[0, :, hs], _CONTRACT_LAST,
                                 preferred_element_type=F32)
            s = lg + sc_ref[tile(kt), :]
            lg_ref[tile(kt), :] = s
            return jnp.maximum(mx, jnp.max(s, axis=0, keepdims=True))

        mx = _pair_loop(nt, pass_a, jnp.full((1, DQ), NEG_BIAS, F32))
        acc_ref[hs, :] = jnp.zeros((dh, DQ), F32)

        def pass_b(kt, den):
            p = jnp.exp2((lg_ref[tile(kt), :] - mx) * c_exp)
            pb = p.astype(BF16)
            pv = jnp.zeros((dh, DQ), F32)
            for j in range(KT // KC):
                pv = pv + jnp.dot(vt_ref[0, kt * (KT // KC) + j, hs, :], pb[j * KC:(j + 1) * KC, :],
                                  preferred_element_type=F32)
            acc_ref[hs, :] += pv
            return den + jnp.sum(p, axis=0, keepdims=True)

        den = _pair_loop(nt, pass_b, jnp.zeros((1, DQ), F32))
        o_ref[0, :, hs] = (acc_ref[hs, :] / den).T.astype(BF16)


def _dsa(qa, ka, va_t, qi_pad, misc, misc_t, dq):
    B, T, D = qa.shape
    dh = D // DSA_HEADS
    KC = va_t.shape[3]
    KT = min(512, T)
    topk = min(IDX_TOPK_MAX, T // 4)
    assert KT >= topk and KT % KC == 0 and dq % CHUNK == 0
    blk = lambda b, i: (b, i, 0)
    whole = functools.partial(pl.BlockSpec, pipeline_mode=pl.Buffered(1))
    return pl.pallas_call(
        functools.partial(_dsa_kernel, T=T, DQ=dq, KT=KT, KC=KC, topk=topk, dh=dh),
        out_shape=jax.ShapeDtypeStruct((B, T, D), BF16),
        grid=(B, T // dq),
        in_specs=[pl.BlockSpec((1, dq, D), blk),
                  whole((1, T, D), lambda b, i: (b, 0, 0)),
                  whole((1, T // KC, D, KC), lambda b, i: (b, 0, 0, 0)),
                  pl.BlockSpec((1, dq, IDX_HEADS * LANES), blk),
                  whole((1, T, MISC_W), lambda b, i: (b, 0, 0)),
                  pl.BlockSpec((1, MISC_W, dq), lambda b, i: (b, 0, i))],
        out_specs=pl.BlockSpec((1, dq, D), blk),
        scratch_shapes=[pltpu.VMEM((T, dq), F32),
                        pltpu.VMEM((T, dq), F32),
                        pltpu.VMEM((D, dq), F32)],
        compiler_params=_params(2),
        name="dsa",
    )(qa, ka, va_t, qi_pad, misc, misc_t)


def _gla_kernel(q_ref, k_ref, v_ref, gb_ref, misc_ref, wg_ref, bg_ref, gn_ref, o_ref, st_ref,
                *, tb, dk, dv):
    @pl.when(pl.program_id(1) == 0)
    def _():
        st_ref[...] = jnp.zeros(st_ref.shape, F32)

    x = jnp.dot(misc_ref[0], wg_ref[...], precision=HI, preferred_element_type=F32) + bg_ref[...]
    log_a = (jnp.minimum(x, 0.0) - jnp.log1p(jnp.exp(-jnp.abs(x)))) / GATE_TAU

    r = lax.broadcasted_iota(I32, (CHUNK, CHUNK), 0)
    c = lax.broadcasted_iota(I32, (CHUNK, CHUNK), 1)
    causal = r >= c
    tril = jnp.where(causal, 1.0, 0.0)
    gn = gn_ref[...]
    contract_last = (((1,), (1,)), ((), ()))
    contract_first = (((0,), (0,)), ((), ()))

    for ci in range(tb // CHUNK):
        rs = slice(ci * CHUNK, (ci + 1) * CHUNK)
        g_all = jnp.dot(tril, log_a[rs], precision=HI, preferred_element_type=F32)
        for h in range(GLA_HEADS):
            ks = slice(h * dk, (h + 1) * dk)
            vs = slice(h * dv, (h + 1) * dv)
            g = g_all[:, ks]
            g_last = g[CHUNK - 1:CHUNK, :]
            qe = (q_ref[0, rs, ks].astype(F32) * (dk ** -0.5)) * jnp.exp(g)
            kh = k_ref[0, rs, ks].astype(F32)
            vh = v_ref[0, rs, vs]
            a = lax.dot_general(qe.astype(BF16), (kh * jnp.exp(-g)).astype(BF16), contract_last,
                                preferred_element_type=F32)
            a = jnp.where(causal, a, 0.0)
            st = st_ref[h]
            o = (jnp.dot(a.astype(BF16), vh, preferred_element_type=F32)
                 + lax.dot_general(qe.astype(BF16), st.astype(BF16), contract_last,
                                   preferred_element_type=F32))
            upd = lax.dot_general(vh, (kh * jnp.exp(g_last - g)).astype(BF16), contract_first,
                                  preferred_element_type=F32)
            st_ref[h] = st * jnp.exp(g_last) + upd
            y = o * lax.rsqrt(jnp.mean(o * o, axis=-1, keepdims=True) + EPS) * gn
            gate = gb_ref[0, rs, vs].astype(F32)
            o_ref[0, rs, vs] = (y * (gate * jax.nn.sigmoid(gate))).astype(BF16)


def _gla(qb, kb, vb, gb, misc, wg_pad, b_gate, gla_norm_g, tb):
    B, T, HK = qb.shape
    HV = vb.shape[2]
    dk, dv = HK // GLA_HEADS, HV // GLA_HEADS
    blk = lambda b, i: (b, i, 0)
    const = lambda b, i: (0, 0)
    return pl.pallas_call(
        functools.partial(_gla_kernel, tb=tb, dk=dk, dv=dv),
        out_shape=jax.ShapeDtypeStruct((B, T, HV), BF16),
        grid=(B, T // tb),
        in_specs=[pl.BlockSpec((1, tb, HK), blk),
                  pl.BlockSpec((1, tb, HK), blk),
                  pl.BlockSpec((1, tb, HV), blk),
                  pl.BlockSpec((1, tb, HV), blk),
                  pl.BlockSpec((1, tb, MISC_W), blk),
                  pl.BlockSpec((MISC_W, HK), const),
                  pl.BlockSpec((1, HK), const),
                  pl.BlockSpec((1, dv), const)],
        out_specs=pl.BlockSpec((1, tb, HV), blk),
        scratch_shapes=[pltpu.VMEM((GLA_HEADS, dv, dk), F32)],
        compiler_params=_params(2),
        name="gla",
    )(qb, kb, vb, gb, misc, wg_pad, b_gate.reshape(1, -1), gla_norm_g.reshape(1, -1))


def _merge_kernel(ya_ref, yb_ref, ga_ref, gbm_ref, x_ref, gt_ref, sc_ref, sh_ref, g2_ref,
                  wa_ref, wb_ref, wo_ref, rw_ref, rb_ref,
                  x1_ref, h2_ref, e_ref, w_ref, cnt_ref, *, n_exp):
    pa = jnp.dot(ya_ref[0], wa_ref[...], preferred_element_type=F32)
    pb = jnp.dot(yb_ref[0], wb_ref[...], preferred_element_type=F32)
    merged = (jax.nn.sigmoid(ga_ref[0].astype(F32)) * pa
              + jax.nn.sigmoid(gbm_ref[0].astype(F32)) * pb)
    mo = jnp.dot(merged.astype(BF16), wo_ref[...], preferred_element_type=F32)
    x1 = x_ref[0] + gt_ref[0] * mo
    x1_ref[0] = x1
    h2 = _rms_mod(x1, g2_ref[...], sc_ref[0], sh_ref[0])
    h2_ref[0] = h2

    logits = jnp.dot(h2, rw_ref[...], precision=HI, preferred_element_type=F32) + rb_ref[...]
    tm = logits.shape[0]
    lane = lax.broadcasted_iota(I32, (tm, LANES), 1)
    logits = jnp.where(lane < n_exp, logits, -jnp.inf)
    top_v, top_e = [], []
    for _ in range(TOP_K):
        mx = jnp.max(logits, axis=1, keepdims=True)
        idx = jnp.min(jnp.where(logits == mx, lane, LANES), axis=1, keepdims=True)
        logits = jnp.where(lane == idx, -jnp.inf, logits)
        top_v.append(mx)
        top_e.append(idx)
    ex = [jnp.exp(v - top_v[0]) for v in top_v]
    den = ex[0]
    for t in ex[1:]:
        den = den + t
    e_out = jnp.zeros((tm, LANES), I32)
    w_out = jnp.zeros((tm, LANES), F32)
    hot = jnp.zeros((tm, LANES), F32)
    for k in range(TOP_K):
        e_out = jnp.where(lane == k, top_e[k], e_out)
        w_out = jnp.where(lane == k, ex[k] / den, w_out)
        hot = hot + jnp.where(lane == top_e[k], 1.0, 0.0)
    e_ref[0] = e_out
    w_ref[0] = w_out

    @pl.when(jnp.logical_and(pl.program_id(0) == 0, pl.program_id(1) == 0))
    def _():
        cnt_ref[...] = jnp.zeros(cnt_ref.shape, F32)

    cnt_ref[0:1, :] += jnp.sum(hot, axis=0, keepdims=True)


def _merge(ya, yb, ga, gbm, x, gt1, sc2, sh2, g2, wa, wb, wo, rw_pad, rb_pad, n_exp, tm):
    B, T, D = x.shape
    row = lambda b, i: (b, i, 0)
    per_b = lambda b, i: (b, 0, 0)
    const = lambda b, i: (0, 0)
    resident = functools.partial(pl.BlockSpec, index_map=const, pipeline_mode=pl.Buffered(1))
    return pl.pallas_call(
        functools.partial(_merge_kernel, n_exp=n_exp),
        out_shape=[jax.ShapeDtypeStruct((B, T, D), F32),
                   jax.ShapeDtypeStruct((B, T, D), F32),
                   jax.ShapeDtypeStruct((B, T, LANES), I32),
                   jax.ShapeDtypeStruct((B, T, LANES), F32),
                   jax.ShapeDtypeStruct((8, LANES), F32)],
        grid=(B, T // tm),
        in_specs=[pl.BlockSpec((1, tm, D), row),
                  pl.BlockSpec((1, tm, D), row),
                  pl.BlockSpec((1, tm, D), row),
                  pl.BlockSpec((1, tm, D), row),
                  pl.BlockSpec((1, tm, D), row),
                  pl.BlockSpec((1, 1, D), per_b),
                  pl.BlockSpec((1, 1, D), per_b),
                  pl.BlockSpec((1, 1, D), per_b),
                  pl.BlockSpec((1, D), const),
                  resident((D, D)), resident((D, D)), resident((D, D)),
                  resident((D, LANES)), pl.BlockSpec((1, LANES), const)],
        out_specs=[pl.BlockSpec((1, tm, D), row),
                   pl.BlockSpec((1, tm, D), row),
                   pl.BlockSpec((1, tm, LANES), row),
                   pl.BlockSpec((1, tm, LANES), row),
                   pl.BlockSpec((8, LANES), const)],
        compiler_params=_params(2),
        name="merge",
    )(ya, yb, ga, gbm, x, gt1, sc2, sh2, g2, wa, wb, wo, rw_pad, rb_pad)


def _plan_kernel(e_ref, cnt_ref, slot_ref, blk_ref, nblk_ref, base_ref, *, n_exp, tmb, nbp):
    tt = e_ref.shape[0]
    lane1 = lax.broadcasted_iota(I32, (1, LANES), 1)

    @pl.when(pl.program_id(0) == 0)
    def _():
        cnt = cnt_ref[0:1, :]
        padded = jnp.where(lane1 < n_exp, jnp.ceil(cnt / tmb) * tmb, 0.0)
        r = lax.broadcasted_iota(I32, (LANES, LANES), 0)
        c = lax.broadcasted_iota(I32, (LANES, LANES), 1)
        upper = jnp.where(r < c, 1.0, 0.0)
        start = jnp.dot(jnp.broadcast_to(padded, (8, LANES)), upper, precision=HI,
                        preferred_element_type=F32)[0:1, :]
        base_ref[...] = start
        end = start + padded
        jrow = lax.broadcasted_iota(I32, (nbp, LANES), 0).astype(F32) * tmb
        owner = jnp.sum(jnp.where(jnp.logical_and(end <= jrow, lane1 < n_exp), 1.0, 0.0),
                        axis=1, keepdims=True)
        blk_ref[...] = jnp.broadcast_to(jnp.minimum(owner, n_exp - 1.0), (nbp, LANES)).astype(I32)
        total = jnp.sum(padded, axis=1, keepdims=True)
        nblk_ref[...] = jnp.broadcast_to(total / tmb, (8, LANES)).astype(I32)

    e = e_ref[...]
    lane = lax.broadcasted_iota(I32, (tt, LANES), 1)
    hot = jnp.zeros((tt, LANES), F32)
    for k in range(TOP_K):
        hot = hot + jnp.where(lane == e[:, k:k + 1], 1.0, 0.0)
    r = lax.broadcasted_iota(I32, (tt, tt), 0)
    c = lax.broadcasted_iota(I32, (tt, tt), 1)
    lower = jnp.where(r > c, 1.0, 0.0).astype(BF16)
    rank = jnp.dot(lower, hot.astype(BF16), preferred_element_type=F32)
    pos = base_ref[...] + rank
    out = jnp.zeros((tt, LANES), F32)
    for k in range(TOP_K):
        sk = jnp.sum(jnp.where(lane == e[:, k:k + 1], pos, 0.0), axis=1, keepdims=True)
        out = jnp.where(lane == k, sk, out)
    slot_ref[...] = out.astype(I32)
    base_ref[...] += jnp.sum(hot, axis=0, keepdims=True)


def _plan(e_pad, cnt, n_exp, tmb, nbp, tt):
    N = e_pad.shape[0]
    return pl.pallas_call(
        functools.partial(_plan_kernel, n_exp=n_exp, tmb=tmb, nbp=nbp),
        out_shape=[jax.ShapeDtypeStruct((N, LANES), I32),
                   jax.ShapeDtypeStruct((nbp, LANES), I32),
                   jax.ShapeDtypeStruct((8, LANES), I32)],
        grid=(N // tt,),
        in_specs=[pl.BlockSpec((tt, LANES), lambda i: (i, 0)),
                  pl.BlockSpec((8, LANES), lambda i: (0, 0))],
        out_specs=[pl.BlockSpec((tt, LANES), lambda i: (i, 0)),
                   pl.BlockSpec((nbp, LANES), lambda i: (0, 0)),
                   pl.BlockSpec((8, LANES), lambda i: (0, 0))],
        scratch_shapes=[pltpu.VMEM((1, LANES), F32)],
        compiler_params=_params(1),
        name="moe_plan",
    )(e_pad, cnt)


def _dispatch_kernel(slot_ref, h_ref, xs_in_ref, xs_ref, sem, *, tt):
    del xs_in_ref

    def row_copy(t, k):
        s = slot_ref[t * TOP_K + k]
        return pltpu.make_async_copy(h_ref.at[pl.ds(t, 1)], xs_ref.at[pl.ds(s, 1)], sem)

    def start(t, carry):
        for k in range(TOP_K):
            row_copy(t, k).start()
        return carry

    def wait(t, carry):
        for k in range(TOP_K):
            row_copy(t, k).wait()
        return carry

    lax.fori_loop(0, tt, start, 0)
    lax.fori_loop(0, tt, wait, 0)


def _dispatch(slots_flat, h2, xs0, tt):
    N, D = h2.shape
    return pl.pallas_call(
        functools.partial(_dispatch_kernel, tt=tt),
        out_shape=jax.ShapeDtypeStruct(xs0.shape, xs0.dtype),
        grid=(N // tt,),
        in_specs=[pl.BlockSpec((tt * TOP_K,), lambda i: (i,), memory_space=pltpu.SMEM),
                  pl.BlockSpec((tt, D), lambda i: (i, 0)),
                  pl.BlockSpec(memory_space=pl.ANY)],
        out_specs=pl.BlockSpec(memory_space=pl.ANY),
        scratch_shapes=[pltpu.SemaphoreType.DMA],
        input_output_aliases={2: 0},
        compiler_params=_params(1),
        name="moe_dispatch",
    )(slots_flat, h2, xs0)


def _experts_kernel(blk_ref, nblk_ref, xs_ref, wgu_ref, bgu_ref, wd_ref, bd_ref, ys_ref,
                    wgu_bf, wd_bf, *, ff):
    j = pl.program_id(0)
    e = blk_ref[j]
    fresh = jnp.logical_or(j == 0, e != blk_ref[jnp.maximum(j - 1, 0)])
    active = j < nblk_ref[0]

    @pl.when(jnp.logical_and(active, fresh))
    def _():
        wgu_bf[...] = wgu_ref[0].astype(BF16)
        wd_bf[...] = wd_ref[0].astype(BF16)

    @pl.when(active)
    def _():
        hgu = jnp.dot(xs_ref[...].astype(BF16), wgu_bf[...], preferred_element_type=F32) + bgu_ref[0]
        gate = jnp.minimum(hgu[:, :ff], SWIGLU_LIMIT)
        up = jnp.clip(hgu[:, ff:], -SWIGLU_LIMIT, SWIGLU_LIMIT)
        act = (up + 1.0) * (gate * jax.nn.sigmoid(SWIGLU_ALPHA * gate))
        ys_ref[...] = jnp.dot(act.astype(BF16), wd_bf[...], preferred_element_type=F32) + bd_ref[0]

    @pl.when(jnp.logical_not(active))
    def _():
        ys_ref[...] = jnp.zeros(ys_ref.shape, F32)


def _experts(blk_e, nblk, xs, w_gate_up, b_gate_up, w_down, b_down, tmb):
    P, D = xs.shape
    E, _, ff2 = w_gate_up.shape
    ff = ff2 // 2
    by_expert = lambda j, blk, nb: (blk[j], 0, 0)
    return pl.pallas_call(
        functools.partial(_experts_kernel, ff=ff),
        out_shape=jax.ShapeDtypeStruct((P, D), F32),
        grid_spec=pltpu.PrefetchScalarGridSpec(
            num_scalar_prefetch=2,
            grid=(P // tmb,),
            in_specs=[pl.BlockSpec((tmb, D), lambda j, blk, nb: (j, 0)),
                      pl.BlockSpec((1, D, ff2), by_expert),
                      pl.BlockSpec((1, 1, ff2), by_expert),
                      pl.BlockSpec((1, ff, D), by_expert),
                      pl.BlockSpec((1, 1, D), by_expert)],
            out_specs=pl.BlockSpec((tmb, D), lambda j, blk, nb: (j, 0)),
            scratch_shapes=[pltpu.VMEM((D, ff2), BF16), pltpu.VMEM((ff, D), BF16)]),
        compiler_params=_params(1),
        name="moe_experts",
    )(blk_e, nblk, xs, w_gate_up, b_gate_up.reshape(E, 1, ff2), w_down, b_down.reshape(E, 1, D))


def _combine_kernel(slot_ref, ys_ref, w_ref, x1_ref, gt_ref, fg_ref, o_ref, buf, sem, *, tt):
    def row_copy(t, k):
        s = slot_ref[t * TOP_K + k]
        return pltpu.make_async_copy(ys_ref.at[pl.ds(s, 1)], buf.at[k, pl.ds(t, 1)], sem)

    def start(t, carry):
        for k in range(TOP_K):
            row_copy(t, k).start()
        return carry

    def wait(t, carry):
        for k in range(TOP_K):
            row_copy(t, k).wait()
        return carry

    lax.fori_loop(0, tt, start, 0)
    lax.fori_loop(0, tt, wait, 0)

    w = w_ref[0]
    moe = buf[0] * w[:, 0:1]
    for k in range(1, TOP_K):
        moe = moe + buf[k] * w[:, k:k + 1]
    x2 = x1_ref[0] + gt_ref[0] * moe
    y = x2 * lax.rsqrt(jnp.mean(x2 * x2, axis=-1, keepdims=True) + EPS)
    o_ref[0] = y * fg_ref[...]


def _combine(slots_flat, ys, w_pad, x1, gt2, final_g, tt):
    B, T, D = x1.shape
    nt = T // tt
    row = lambda b, i: (b, i, 0)
    return pl.pallas_call(
        functools.partial(_combine_kernel, tt=tt),
        out_shape=jax.ShapeDtypeStruct((B, T, D), F32),
        grid=(B, nt),
        in_specs=[pl.BlockSpec((tt * TOP_K,), lambda b, i: (b * nt + i,), memory_space=pltpu.SMEM),
                  pl.BlockSpec(memory_space=pl.ANY),
                  pl.BlockSpec((1, tt, LANES), row),
                  pl.BlockSpec((1, tt, D), row),
                  pl.BlockSpec((1, 1, D), lambda b, i: (b, 0, 0)),
                  pl.BlockSpec((1, D), lambda b, i: (0, 0))],
        out_specs=pl.BlockSpec((1, tt, D), row),
        scratch_shapes=[pltpu.VMEM((TOP_K, tt, D), F32), pltpu.SemaphoreType.DMA],
        compiler_params=_params(2),
        name="moe_combine",
    )(slots_flat, ys, w_pad, x1, gt2, final_g.reshape(1, -1))


def _layer(x, c, w_ada, b_ada, norm1_g, w_in, w_gate_lr, b_gate, gla_norm_g, w_branch_a,
           w_branch_b, w_out, norm2_g, router_w, router_b, w_gate_up, b_gate_up, w_down, b_down,
           final_g):
    B, T, D = x.shape
    N = B * T
    n_exp = router_w.shape[1]
    hk = w_gate_lr.shape[1]
    hv = w_branch_b.shape[0]
    ni = IDX_HEADS * IDX_DIM

    c8 = jnp.zeros((8, D), F32).at[:B].set(c)
    mod = _adaln(c8, w_ada, b_ada)[:B]
    sh1, sc1, gt1, sh2, sc2, gt2 = [m.reshape(B, 1, D) for m in jnp.split(mod, 6, axis=-1)]

    sizes = (D, D, D, ni, IDX_DIM, IDX_HEADS, hk, hk, hv, hv, GATE_RANK, D, D)
    offs = [0]
    for s in sizes:
        offs.append(offs[-1] + s)
    seg = lambda j: w_in[:, offs[j]:offs[j + 1]]
    w_qi = jnp.pad(seg(3).reshape(D, IDX_HEADS, IDX_DIM),
                   ((0, 0), (0, 0), (0, LANES - IDX_DIM))).reshape(D, IDX_HEADS * LANES)
    groups = [seg(0), seg(1), w_qi, seg(6), seg(7), seg(8), seg(9), seg(11), seg(12)]
    widths = tuple(g.shape[1] for g in groups)
    w_main = jnp.concatenate(groups, axis=1).astype(BF16)
    w_misc = jnp.concatenate(
        [seg(4), seg(5), seg(10), jnp.zeros((D, MISC_W - IDX_DIM - IDX_HEADS - GATE_RANK), F32)],
        axis=1)
    qa, ka, qi, qb, kb, vb, gb, ga, gbm, va_t, misc, misc_t = _inproj(
        x, norm1_g.reshape(1, D), sc1, sh1, w_main, seg(2).T.astype(BF16), w_misc, widths,
        min(256, T))

    ya = _dsa(qa, ka, va_t, qi, misc, misc_t, min(512, T))

    wg_pad = jnp.zeros((MISC_W, hk), F32).at[LR_OFF:LR_OFF + GATE_RANK].set(w_gate_lr)
    yb = _gla(qb, kb, vb, gb, misc, wg_pad, b_gate, gla_norm_g, min(256, T))

    rw_pad = jnp.zeros((D, LANES), F32).at[:, :n_exp].set(router_w)
    rb_pad = jnp.zeros((1, LANES), F32).at[0, :n_exp].set(router_b)
    x1, h2, e_pad, w_pad, cnt = _merge(
        ya, yb, ga, gbm, x, gt1, sc2, sh2, norm2_g.reshape(1, D),
        w_branch_a.astype(BF16), w_branch_b.astype(BF16), w_out.astype(BF16),
        rw_pad, rb_pad, n_exp, min(512, T))

    tmb = 256
    nb = -(-N * TOP_K // tmb) + n_exp
    nbp = -(-nb // 8) * 8
    slots, blk_e, nblk = _plan(e_pad.reshape(N, LANES), cnt, n_exp, tmb, nbp, min(512, N))

    tt_d = min(256, T)
    slots_flat = slots[:, :TOP_K].reshape(-1)
    xs = _dispatch(slots_flat, h2.reshape(N, D), jnp.zeros((nb * tmb, D), F32), tt_d)
    ys = _experts(blk_e[:nb, 0], nblk[0, :1], xs, w_gate_up, b_gate_up, w_down, b_down, tmb)
    tt_c = min(128, T)
    return _combine(slots_flat, ys, w_pad, x1, gt2, final_g, tt_c)


def kernel(x, c, w_ada, b_ada, norm1_g, w_in, w_gate_lr, b_gate, gla_norm_g, w_branch_a, w_branch_b,
           w_out, norm2_g, router_w, router_b, w_gate_up, b_gate_up, w_down, b_down, final_g):
    assert w_ada.shape[0] == 1, "single-layer block"
    return _layer(x, c, w_ada[0], b_ada[0], norm1_g[0], w_in[0], w_gate_lr[0], b_gate[0],
                  gla_norm_g[0], w_branch_a[0], w_branch_b[0], w_out[0], norm2_g[0], router_w[0],
                  router_b[0], w_gate_up[0], b_gate_up[0], w_down[0], b_down[0], final_g)
```

```python
import functools

import jax
import jax.numpy as jnp
from jax import lax
from jax.experimental import pallas as pl
from jax.experimental.pallas import tpu as pltpu

CHUNK = 64
Q_BLOCK = 128
DSA_HEADS = 8
IDX_HEADS = 8
IDX_DIM = 64
IDX_TOPK_MAX = 256
GLA_HEADS = 4
GATE_RANK = 16
GATE_TAU = 16.0
TOP_K = 4
SWIGLU_LIMIT = 7.0
SWIGLU_ALPHA = 1.702
EPS = 1e-6

LANES = 128
MISC_W = LANES
KI_OFF, WI_OFF, LR_OFF = 0, IDX_DIM, IDX_DIM + IDX_HEADS
NEG_SCORE = -3.0e38
NEG_BIAS = -1.0e30
VMEM_LIMIT = 56 * 1024 * 1024

F32 = jnp.float32
BF16 = jnp.bfloat16
I32 = jnp.int32
HI = lax.Precision.HIGHEST


def _params(n_axes, vmem=VMEM_LIMIT):
    return pltpu.CompilerParams(dimension_semantics=("arbitrary",) * n_axes,
                                vmem_limit_bytes=vmem)


def _log2(n):
    b = n.bit_length() - 1
    assert (1 << b) == n, n
    return b


def _adaln_kernel(c_ref, w_ref, b_ref, o_ref):
    c = c_ref[...]
    s = c * jax.nn.sigmoid(c)
    o_ref[...] = jnp.dot(s, w_ref[...], precision=HI, preferred_element_type=F32) + b_ref[...]


def _adaln(c8, w_ada, b_ada):
    D = c8.shape[1]
    n = w_ada.shape[1] // D
    return pl.pallas_call(
        _adaln_kernel,
        out_shape=jax.ShapeDtypeStruct((c8.shape[0], n * D), F32),
        grid=(n,),
        in_specs=[pl.BlockSpec((c8.shape[0], D), lambda j: (0, 0)),
                  pl.BlockSpec((D, D), lambda j: (0, j)),
                  pl.BlockSpec((1, D), lambda j: (0, j))],
        out_specs=pl.BlockSpec((c8.shape[0], D), lambda j: (0, j)),
        compiler_params=_params(1),
        name="adaln",
    )(c8, w_ada, b_ada.reshape(1, -1))


def _rms_mod(x, g, sc, sh):
    y = x * lax.rsqrt(jnp.mean(x * x, axis=-1, keepdims=True) + EPS)
    return (y * g) * (1.0 + sc) + sh


_CONTRACT_LAST = (((1,), (1,)), ((), ()))
_CONTRACT_FIRST = (((0,), (0,)), ((), ()))


def _inproj_kernel(x_ref, g_ref, sc_ref, sh_ref, w_ref, wvt_ref, wm_ref, wmt_ref, *out_refs, widths):
    h = _rms_mod(x_ref[0], g_ref[...], sc_ref[0], sh_ref[0])
    hb = h.astype(BF16)
    out_refs[-1][0] = lax.dot_general(wmt_ref[...], h, _CONTRACT_LAST, precision=HI,
                                      preferred_element_type=F32)
    out_refs[-2][0] = jnp.dot(h, wm_ref[...], precision=HI, preferred_element_type=F32)
    out_refs[-3][0, 0] = lax.dot_general(wvt_ref[...], hb, _CONTRACT_LAST,
                                         preferred_element_type=F32).astype(BF16)
    off = 0
    for o_ref, w in zip(out_refs[:-3], widths):
        o_ref[0] = jnp.dot(hb, w_ref[:, off:off + w], preferred_element_type=F32).astype(BF16)
        off += w


def _inproj(x, g, sc, sh, w_main, w_vt, w_misc, widths, tm):
    B, T, D = x.shape
    wtot = w_main.shape[1]
    dv = w_vt.shape[0]
    row = lambda b, i: (b, i, 0)
    per_b = lambda b, i: (b, 0, 0)
    const = lambda b, i: (0, 0)
    resident = functools.partial(pl.BlockSpec, index_map=const, pipeline_mode=pl.Buffered(1))
    out_shape = [jax.ShapeDtypeStruct((B, T, w), BF16) for w in widths]
    out_specs = [pl.BlockSpec((1, tm, w), row) for w in widths]
    out_shape += [jax.ShapeDtypeStruct((B, T // tm, dv, tm), BF16),
                  jax.ShapeDtypeStruct((B, T, MISC_W), F32),
                  jax.ShapeDtypeStruct((B, MISC_W, T), F32)]
    out_specs += [pl.BlockSpec((1, 1, dv, tm), lambda b, i: (b, i, 0, 0)),
                  pl.BlockSpec((1, tm, MISC_W), row),
                  pl.BlockSpec((1, MISC_W, tm), lambda b, i: (b, 0, i))]
    return pl.pallas_call(
        functools.partial(_inproj_kernel, widths=widths),
        out_shape=out_shape,
        grid=(B, T // tm),
        in_specs=[pl.BlockSpec((1, tm, D), row),
                  pl.BlockSpec((1, D), const),
                  pl.BlockSpec((1, 1, D), per_b),
                  pl.BlockSpec((1, 1, D), per_b),
                  resident((D, wtot)), resident((dv, D)),
                  resident((D, MISC_W)), resident((MISC_W, D))],
        out_specs=out_specs,
        compiler_params=_params(2),
        name="inproj",
    )(x, g, sc, sh, w_main, w_vt, w_misc, w_misc.T)


def _pair_loop(n, body, init):
    def two(j, carry):
        return body(2 * j + 1, body(2 * j, carry))
    carry = lax.fori_loop(0, n >> 1, two, init)
    return lax.cond((n & 1) == 1, lambda c: body(n - 1, c), lambda c: c, carry)


def _dsa_kernel(q_ref, k_ref, vt_ref, qi_ref, mall_ref, mt_ref, o_ref, sc_ref, lg0_ref, lg1_ref,
                acc_ref, *, T, DQ, KT, KC, topk, dh):
    i = pl.program_id(1)
    lg_refs = (lg0_ref, lg1_ref)
    nt = ((i + 1) * DQ + KT - 1) // KT
    qpos = i * DQ + lax.broadcasted_iota(I32, (1, DQ), 1)
    lim = ((qpos >> _log2(CHUNK)) + 1) << _log2(CHUNK)
    krow = lax.broadcasted_iota(I32, (KT, DQ), 0)

    def tile(kt):
        return pl.ds(pl.multiple_of(kt * KT, KT), KT)

    def half_tile(kt):
        return pl.ds(pl.multiple_of(kt * (KT // 2), KT // 2), KT // 2)

    def trunc_bf16(x):
        bits = lax.bitcast_convert_type(x, I32) & jnp.int32(-65536)
        return lax.bitcast_convert_type(bits, F32).astype(BF16)

    wi_t = mt_ref[0, WI_OFF:WI_OFF + IDX_HEADS, :] * ((IDX_HEADS ** -0.5) * (IDX_DIM ** -0.5))

    def score_tile(kt, carry):
        ki_t = mall_ref[0, tile(kt), :].astype(BF16)
        s = jnp.zeros((KT, DQ), F32)
        for h in range(IDX_HEADS):
            x = lax.dot_general(ki_t, qi_ref[0, :, h * LANES:(h + 1) * LANES], _CONTRACT_LAST,
                                preferred_element_type=F32)
            s = s + jnp.maximum(x, 0.0) * wi_t[h:h + 1, :]
        s = jnp.where(krow + kt * KT < lim, s, NEG_SCORE)
        sc_ref[tile(kt), :] = s
        lg1_ref[half_tile(kt), :] = pltpu.bitcast(trunc_bf16(s), F32)
        return carry

    lax.fori_loop(0, nt, score_tile, 0)

    def count(pred):
        def body(kt, acc):
            m = jnp.where(pred(sc_ref[tile(kt), :], kt * KT), 1.0, 0.0)
            return acc + jnp.sum(m, axis=0, keepdims=True)
        return lax.fori_loop(0, nt, body, jnp.zeros((1, DQ), F32))

    int_min = jnp.int32(-2 ** 31)

    def key_to_float(u):
        bits = jnp.where(u < 0, u ^ int_min, ~u)
        return lax.bitcast_convert_type(bits, F32)

    one_b, zero_b = jnp.ones((), BF16), jnp.zeros((), BF16)
    rows_b = 16

    def count_hi(tb):
        def body(kt, acc):
            hi = pltpu.bitcast(lg1_ref[half_tile(kt), :], BF16)
            m = jnp.where(hi >= tb, one_b, zero_b)
            for j in range(KT // rows_b):
                acc = acc + m[j * rows_b:(j + 1) * rows_b, :]
            return acc
        acc = lax.fori_loop(0, nt, body, jnp.zeros((rows_b, DQ), BF16))
        return jnp.sum(acc.astype(F32), axis=0, keepdims=True)

    def hi_body(it, carry):
        u, c_ge = carry
        trial = u | (jnp.int32(1) << (31 - it))
        cnt = count_hi(trunc_bf16(key_to_float(trial)))
        ok = cnt >= topk
        return jnp.where(ok, trial, u), jnp.where(ok, cnt, c_ge)

    def lo_body(it, carry):
        u, c_ge = carry
        trial = u | (jnp.int32(1) << (15 - it))
        t = key_to_float(trial)
        cnt = count(lambda s, c0: s >= t)
        ok = cnt >= topk
        return jnp.where(ok, trial, u), jnp.where(ok, cnt, c_ge)

    carry = (jnp.zeros((1, DQ), I32), jnp.zeros((1, DQ), F32))
    carry = lax.fori_loop(0, 16, hi_body, carry)
    u, c_ge = lax.fori_loop(0, 16, lo_body, carry)
    thr = key_to_float(u)

    c_gt = count(lambda s, c0: s > thr)
    need = topk - c_gt
    excess = jnp.logical_and(c_ge - c_gt > need, thr > NEG_SCORE)

    def tie_search():
        def body(it, v):
            trial = v | (jnp.int32(1) << (_log2(T) - 1 - it))
            cnt = count(lambda s, c0: jnp.logical_and(s == thr, krow + c0 < trial))
            return jnp.where(cnt < need, trial, v)
        v = lax.fori_loop(0, _log2(T), body, jnp.zeros((1, DQ), I32))
        return jnp.where(excess, v, T)

    has_excess = jnp.max(jnp.where(excess, 1.0, 0.0)) > 0.0
    cut = lax.cond(has_excess, tie_search, lambda: jnp.full((1, DQ), T, I32))

    def bias_tile(kt, carry):
        s = sc_ref[tile(kt), :]
        key = krow + kt * KT
        sel = jnp.logical_or(s > thr, jnp.logical_and(s == thr, key <= cut))
        sel = jnp.logical_and(sel, key < lim)
        sc_ref[tile(kt), :] = jnp.where(sel, 0.0, NEG_BIAS)
        return carry

    lax.fori_loop(0, nt, bias_tile, 0)

    c_exp = (dh ** -0.5) * 1.4426950408889634

    def pass_a(h, kt, mx):
        hs = slice(h * dh, (h + 1) * dh)
        lg = lax.dot_general(k_ref[0, tile(kt), hs], q_ref[0, :, hs], _CONTRACT_LAST,
                             preferred_element_type=F32)
        s = lg + sc_ref[tile(kt), :]
        lg_refs[h % 2][tile(kt), :] = s
        return jnp.maximum(mx, jnp.max(s, axis=0, keepdims=True))

    def pass_b(h, kt, mx, den):
        hs = slice(h * dh, (h + 1) * dh)
        p = jnp.exp2((lg_refs[h % 2][tile(kt), :] - mx) * c_exp)
        pb = p.astype(BF16)
        pv = jnp.zeros((dh, DQ), F32)
        for j in range(KT // KC):
            pv = pv + jnp.dot(vt_ref[0, kt * (KT // KC) + j, hs, :], pb[j * KC:(j + 1) * KC, :],
                              preferred_element_type=F32)
        acc_ref[hs, :] += pv
        return den + jnp.sum(p, axis=0, keepdims=True)

    mx0 = jnp.full((1, DQ), NEG_BIAS, F32)
    den0 = jnp.zeros((1, DQ), F32)
    acc_ref[...] = jnp.zeros(acc_ref.shape, F32)
    mx = _pair_loop(nt, functools.partial(pass_a, 0), mx0)
    for h in range(DSA_HEADS):
        if h + 1 < DSA_HEADS:
            def both(kt, carry, h=h, mx=mx):
                den, mx_next = carry
                return pass_b(h, kt, mx, den), pass_a(h + 1, kt, mx_next)
            den, mx_next = _pair_loop(nt, both, (den0, mx0))
        else:
            den = _pair_loop(nt, lambda kt, den, h=h, mx=mx: pass_b(h, kt, mx, den), den0)
            mx_next = None
        hs = slice(h * dh, (h + 1) * dh)
        o_ref[0, :, hs] = (acc_ref[hs, :] / den).T.astype(BF16)
        mx = mx_next


def _dsa(qa, ka, va_t, qi_pad, misc, misc_t, dq):
    B, T, D = qa.shape
    dh = D // DSA_HEADS
    KC = va_t.shape[3]
    KT = min(512, T)
    topk = min(IDX_TOPK_MAX, T // 4)
    assert KT >= topk and KT % KC == 0 and dq % CHUNK == 0
    assert T // 16 <= 256, "packed bf16 counters must stay exact"
    blk = lambda b, i: (b, i, 0)
    whole = functools.partial(pl.BlockSpec, pipeline_mode=pl.Buffered(1))
    return pl.pallas_call(
        functools.partial(_dsa_kernel, T=T, DQ=dq, KT=KT, KC=KC, topk=topk, dh=dh),
        out_shape=jax.ShapeDtypeStruct((B, T, D), BF16),
        grid=(B, T // dq),
        in_specs=[pl.BlockSpec((1, dq, D), blk),
                  whole((1, T, D), lambda b, i: (b, 0, 0)),
                  whole((1, T // KC, D, KC), lambda b, i: (b, 0, 0, 0)),
                  pl.BlockSpec((1, dq, IDX_HEADS * LANES), blk),
                  whole((1, T, MISC_W), lambda b, i: (b, 0, 0)),
                  pl.BlockSpec((1, MISC_W, dq), lambda b, i: (b, 0, i))],
        out_specs=pl.BlockSpec((1, dq, D), blk),
        scratch_shapes=[pltpu.VMEM((T, dq), F32),
                        pltpu.VMEM((T, dq), F32),
                        pltpu.VMEM((T, dq), F32),
                        pltpu.VMEM((D, dq), F32)],
        compiler_params=_params(2),
        name="dsa",
    )(qa, ka, va_t, qi_pad, misc, misc_t)


def _gla_kernel(q_ref, k_ref, v_ref, gb_ref, misc_ref, wg_ref, bg_ref, gn_ref, o_ref, st_ref,
                *, tb, dk, dv):
    @pl.when(pl.program_id(1) == 0)
    def _():
        st_ref[...] = jnp.zeros(st_ref.shape, F32)

    x = jnp.dot(misc_ref[0], wg_ref[...], precision=HI, preferred_element_type=F32) + bg_ref[...]
    log_a = (jnp.minimum(x, 0.0) - jnp.log1p(jnp.exp(-jnp.abs(x)))) / GATE_TAU

    r = lax.broadcasted_iota(I32, (CHUNK, CHUNK), 0)
    c = lax.broadcasted_iota(I32, (CHUNK, CHUNK), 1)
    causal = r >= c
    tril = jnp.where(causal, 1.0, 0.0)
    gn = gn_ref[...]
    contract_last = (((1,), (1,)), ((), ()))
    contract_first = (((0,), (0,)), ((), ()))

    for ci in range(tb // CHUNK):
        rs = slice(ci * CHUNK, (ci + 1) * CHUNK)
        g_all = jnp.dot(tril, log_a[rs], precision=HI, preferred_element_type=F32)
        for h in range(GLA_HEADS):
            ks = slice(h * dk, (h + 1) * dk)
            vs = slice(h * dv, (h + 1) * dv)
            g = g_all[:, ks]
            g_last = g[CHUNK - 1:CHUNK, :]
            qe = (q_ref[0, rs, ks].astype(F32) * (dk ** -0.5)) * jnp.exp(g)
            kh = k_ref[0, rs, ks].astype(F32)
            vh = v_ref[0, rs, vs]
            a = lax.dot_general(qe.astype(BF16), (kh * jnp.exp(-g)).astype(BF16), contract_last,
                                preferred_element_type=F32)
            a = jnp.where(causal, a, 0.0)
            st = st_ref[h]
            o = (jnp.dot(a.astype(BF16), vh, preferred_element_type=F32)
                 + lax.dot_general(qe.astype(BF16), st.astype(BF16), contract_last,
                                   preferred_element_type=F32))
            upd = lax.dot_general(vh, (kh * jnp.exp(g_last - g)).astype(BF16), contract_first,
                                  preferred_element_type=F32)
            st_ref[h] = st * jnp.exp(g_last) + upd
            y = o * lax.rsqrt(jnp.mean(o * o, axis=-1, keepdims=True) + EPS) * gn
            gate = gb_ref[0, rs, vs].astype(F32)
            o_ref[0, rs, vs] = (y * (gate * jax.nn.sigmoid(gate))).astype(BF16)


def _gla(qb, kb, vb, gb, misc, wg_pad, b_gate, gla_norm_g, tb):
    B, T, HK = qb.shape
    HV = vb.shape[2]
    dk, dv = HK // GLA_HEADS, HV // GLA_HEADS
    blk = lambda b, i: (b, i, 0)
    const = lambda b, i: (0, 0)
    return pl.pallas_call(
        functools.partial(_gla_kernel, tb=tb, dk=dk, dv=dv),
        out_shape=jax.ShapeDtypeStruct((B, T, HV), BF16),
        grid=(B, T // tb),
        in_specs=[pl.BlockSpec((1, tb, HK), blk),
                  pl.BlockSpec((1, tb, HK), blk),
                  pl.BlockSpec((1, tb, HV), blk),
                  pl.BlockSpec((1, tb, HV), blk),
                  pl.BlockSpec((1, tb, MISC_W), blk),
                  pl.BlockSpec((MISC_W, HK), const),
                  pl.BlockSpec((1, HK), const),
                  pl.BlockSpec((1, dv), const)],
        out_specs=pl.BlockSpec((1, tb, HV), blk),
        scratch_shapes=[pltpu.VMEM((GLA_HEADS, dv, dk), F32)],
        compiler_params=_params(2),
        name="gla",
    )(qb, kb, vb, gb, misc, wg_pad, b_gate.reshape(1, -1), gla_norm_g.reshape(1, -1))


def _pack_bf16_pairs(x):
    half = x.shape[1] // 2
    bits = lax.bitcast_convert_type(x.astype(BF16).astype(F32), I32)
    return (bits[:, half:] & jnp.int32(-65536)) | lax.shift_right_logical(bits[:, :half], 16)


def _unpack_bf16_pairs(w):
    lo = lax.bitcast_convert_type(w << 16, F32)
    hi = lax.bitcast_convert_type(w & jnp.int32(-65536), F32)
    return jnp.concatenate([lo, hi], axis=1)


def _merge_kernel(ya_ref, yb_ref, ga_ref, gbm_ref, x_ref, gt_ref, sc_ref, sh_ref, g2_ref,
                  wa_ref, wb_ref, wo_ref, rw_ref, rb_ref,
                  x1_ref, h2_ref, e_ref, w_ref, cnt_ref, *, n_exp):
    pa = jnp.dot(ya_ref[0], wa_ref[...], preferred_element_type=F32)
    pb = jnp.dot(yb_ref[0], wb_ref[...], preferred_element_type=F32)
    merged = (jax.nn.sigmoid(ga_ref[0].astype(F32)) * pa
              + jax.nn.sigmoid(gbm_ref[0].astype(F32)) * pb)
    mo = jnp.dot(merged.astype(BF16), wo_ref[...], preferred_element_type=F32)
    x1 = x_ref[0] + gt_ref[0] * mo
    x1_ref[0] = x1
    h2 = _rms_mod(x1, g2_ref[...], sc_ref[0], sh_ref[0])
    h2_ref[0] = _pack_bf16_pairs(h2)

    logits = jnp.dot(h2, rw_ref[...], precision=HI, preferred_element_type=F32) + rb_ref[...]
    tm = logits.shape[0]
    lane = lax.broadcasted_iota(I32, (tm, LANES), 1)
    logits = jnp.where(lane < n_exp, logits, -jnp.inf)
    top_v, top_e = [], []
    for _ in range(TOP_K):
        mx = jnp.max(logits, axis=1, keepdims=True)
        idx = jnp.min(jnp.where(logits == mx, lane, LANES), axis=1, keepdims=True)
        logits = jnp.where(lane == idx, -jnp.inf, logits)
        top_v.append(mx)
        top_e.append(idx)
    ex = [jnp.exp(v - top_v[0]) for v in top_v]
    den = ex[0]
    for t in ex[1:]:
        den = den + t
    e_out = jnp.zeros((tm, LANES), I32)
    w_out = jnp.zeros((tm, LANES), F32)
    hot = jnp.zeros((tm, LANES), F32)
    for k in range(TOP_K):
        e_out = jnp.where(lane == k, top_e[k], e_out)
        w_out = jnp.where(lane == k, ex[k] / den, w_out)
        hot = hot + jnp.where(lane == top_e[k], 1.0, 0.0)
    e_ref[0] = e_out
    w_ref[0] = w_out

    @pl.when(jnp.logical_and(pl.program_id(0) == 0, pl.program_id(1) == 0))
    def _():
        cnt_ref[...] = jnp.zeros(cnt_ref.shape, F32)

    cnt_ref[0:1, :] += jnp.sum(hot, axis=0, keepdims=True)


def _merge(ya, yb, ga, gbm, x, gt1, sc2, sh2, g2, wa, wb, wo, rw_pad, rb_pad, n_exp, tm):
    B, T, D = x.shape
    row = lambda b, i: (b, i, 0)
    per_b = lambda b, i: (b, 0, 0)
    const = lambda b, i: (0, 0)
    resident = functools.partial(pl.BlockSpec, index_map=const, pipeline_mode=pl.Buffered(1))
    return pl.pallas_call(
        functools.partial(_merge_kernel, n_exp=n_exp),
        out_shape=[jax.ShapeDtypeStruct((B, T, D), F32),
                   jax.ShapeDtypeStruct((B, T, D // 2), I32),
                   jax.ShapeDtypeStruct((B, T, LANES), I32),
                   jax.ShapeDtypeStruct((B, T, LANES), F32),
                   jax.ShapeDtypeStruct((8, LANES), F32)],
        grid=(B, T // tm),
        in_specs=[pl.BlockSpec((1, tm, D), row),
                  pl.BlockSpec((1, tm, D), row),
                  pl.BlockSpec((1, tm, D), row),
                  pl.BlockSpec((1, tm, D), row),
                  pl.BlockSpec((1, tm, D), row),
                  pl.BlockSpec((1, 1, D), per_b),
                  pl.BlockSpec((1, 1, D), per_b),
                  pl.BlockSpec((1, 1, D), per_b),
                  pl.BlockSpec((1, D), const),
                  resident((D, D)), resident((D, D)), resident((D, D)),
                  resident((D, LANES)), pl.BlockSpec((1, LANES), const)],
        out_specs=[pl.BlockSpec((1, tm, D), row),
                   pl.BlockSpec((1, tm, D // 2), row),
                   pl.BlockSpec((1, tm, LANES), row),
                   pl.BlockSpec((1, tm, LANES), row),
                   pl.BlockSpec((8, LANES), const)],
        compiler_params=_params(2),
        name="merge",
    )(ya, yb, ga, gbm, x, gt1, sc2, sh2, g2, wa, wb, wo, rw_pad, rb_pad)


def _plan_kernel(e_ref, cnt_ref, slot_ref, blk_ref, nblk_ref, base_ref, *, n_exp, tmb, nbp):
    tt = e_ref.shape[0]
    lane1 = lax.broadcasted_iota(I32, (1, LANES), 1)

    @pl.when(pl.program_id(0) == 0)
    def _():
        cnt = cnt_ref[0:1, :]
        padded = jnp.where(lane1 < n_exp, jnp.ceil(cnt / tmb) * tmb, 0.0)
        r = lax.broadcasted_iota(I32, (LANES, LANES), 0)
        c = lax.broadcasted_iota(I32, (LANES, LANES), 1)
        upper = jnp.where(r < c, 1.0, 0.0)
        start = jnp.dot(jnp.broadcast_to(padded, (8, LANES)), upper, precision=HI,
                        preferred_element_type=F32)[0:1, :]
        base_ref[...] = start
        end = start + padded
        jrow = lax.broadcasted_iota(I32, (nbp, LANES), 0).astype(F32) * tmb
        owner = jnp.sum(jnp.where(jnp.logical_and(end <= jrow, lane1 < n_exp), 1.0, 0.0),
                        axis=1, keepdims=True)
        blk_ref[...] = jnp.broadcast_to(jnp.minimum(owner, n_exp - 1.0), (nbp, LANES)).astype(I32)
        total = jnp.sum(padded, axis=1, keepdims=True)
        nblk_ref[...] = jnp.broadcast_to(total / tmb, (8, LANES)).astype(I32)

    e = e_ref[...]
    lane = lax.broadcasted_iota(I32, (tt, LANES), 1)
    hot = jnp.zeros((tt, LANES), F32)
    for k in range(TOP_K):
        hot = hot + jnp.where(lane == e[:, k:k + 1], 1.0, 0.0)
    r = lax.broadcasted_iota(I32, (tt, tt), 0)
    c = lax.broadcasted_iota(I32, (tt, tt), 1)
    lower = jnp.where(r > c, 1.0, 0.0).astype(BF16)
    rank = jnp.dot(lower, hot.astype(BF16), preferred_element_type=F32)
    pos = base_ref[...] + rank
    out = jnp.zeros((tt, LANES), F32)
    for k in range(TOP_K):
        sk = jnp.sum(jnp.where(lane == e[:, k:k + 1], pos, 0.0), axis=1, keepdims=True)
        out = jnp.where(lane == k, sk, out)
    slot_ref[...] = out.astype(I32)
    base_ref[...] += jnp.sum(hot, axis=0, keepdims=True)


def _plan(e_pad, cnt, n_exp, tmb, nbp, tt):
    N = e_pad.shape[0]
    return pl.pallas_call(
        functools.partial(_plan_kernel, n_exp=n_exp, tmb=tmb, nbp=nbp),
        out_shape=[jax.ShapeDtypeStruct((N, LANES), I32),
                   jax.ShapeDtypeStruct((nbp, LANES), I32),
                   jax.ShapeDtypeStruct((8, LANES), I32)],
        grid=(N // tt,),
        in_specs=[pl.BlockSpec((tt, LANES), lambda i: (i, 0)),
                  pl.BlockSpec((8, LANES), lambda i: (0, 0))],
        out_specs=[pl.BlockSpec((tt, LANES), lambda i: (i, 0)),
                   pl.BlockSpec((nbp, LANES), lambda i: (0, 0)),
                   pl.BlockSpec((8, LANES), lambda i: (0, 0))],
        scratch_shapes=[pltpu.VMEM((1, LANES), F32)],
        compiler_params=_params(1),
        name="moe_plan",
    )(e_pad, cnt)


def _dispatch_kernel(slot_ref, h_ref, xs_in_ref, xs_ref, sem, *, tt):
    del xs_in_ref

    def row_copy(t, k):
        s = slot_ref[t * TOP_K + k]
        return pltpu.make_async_copy(h_ref.at[pl.ds(t, 1)], xs_ref.at[pl.ds(s, 1)], sem)

    def start(t, carry):
        for k in range(TOP_K):
            row_copy(t, k).start(priority=k % 2)
        return carry

    def wait(t, carry):
        for k in range(TOP_K):
            row_copy(t, k).wait()
        return carry

    lax.fori_loop(0, tt, start, 0)
    lax.fori_loop(0, tt, wait, 0)


def _dispatch(slots_flat, h2, xs0, tt):
    N, D = h2.shape
    return pl.pallas_call(
        functools.partial(_dispatch_kernel, tt=tt),
        out_shape=jax.ShapeDtypeStruct(xs0.shape, xs0.dtype),
        grid=(N // tt,),
        in_specs=[pl.BlockSpec((tt * TOP_K,), lambda i: (i,), memory_space=pltpu.SMEM),
                  pl.BlockSpec((tt, D), lambda i: (i, 0)),
                  pl.BlockSpec(memory_space=pl.ANY)],
        out_specs=pl.BlockSpec(memory_space=pl.ANY),
        scratch_shapes=[pltpu.SemaphoreType.DMA],
        input_output_aliases={2: 0},
        compiler_params=_params(1),
        name="moe_dispatch",
    )(slots_flat, h2, xs0)


def _experts_kernel(blk_ref, nblk_ref, xs_ref, wgu_ref, bgu_ref, wd_ref, bd_ref, ys_ref,
                    wgu_bf, wd_bf, *, ff):
    j = pl.program_id(0)
    e = blk_ref[j]
    fresh = jnp.logical_or(j == 0, e != blk_ref[jnp.maximum(j - 1, 0)])
    active = j < nblk_ref[0]

    @pl.when(jnp.logical_and(active, fresh))
    def _():
        wgu_bf[...] = wgu_ref[0].astype(BF16)
        wd_bf[...] = wd_ref[0].astype(BF16)

    @pl.when(active)
    def _():
        x = _unpack_bf16_pairs(xs_ref[...]).astype(BF16)
        hgu = jnp.dot(x, wgu_bf[...], preferred_element_type=F32) + bgu_ref[0]
        gate = jnp.minimum(hgu[:, :ff], SWIGLU_LIMIT)
        up = jnp.clip(hgu[:, ff:], -SWIGLU_LIMIT, SWIGLU_LIMIT)
        act = (up + 1.0) * (gate * jax.nn.sigmoid(SWIGLU_ALPHA * gate))
        y = jnp.dot(act.astype(BF16), wd_bf[...], preferred_element_type=F32) + bd_ref[0]
        ys_ref[...] = _pack_bf16_pairs(y)

    @pl.when(jnp.logical_not(active))
    def _():
        ys_ref[...] = jnp.zeros(ys_ref.shape, I32)


def _experts(blk_e, nblk, xs, w_gate_up, b_gate_up, w_down, b_down, tmb):
    P, dp = xs.shape
    E, D, ff2 = w_gate_up.shape
    ff = ff2 // 2
    by_expert = lambda j, blk, nb: (blk[j], 0, 0)
    return pl.pallas_call(
        functools.partial(_experts_kernel, ff=ff),
        out_shape=jax.ShapeDtypeStruct((P, dp), I32),
        grid_spec=pltpu.PrefetchScalarGridSpec(
            num_scalar_prefetch=2,
            grid=(P // tmb,),
            in_specs=[pl.BlockSpec((tmb, dp), lambda j, blk, nb: (j, 0)),
                      pl.BlockSpec((1, D, ff2), by_expert),
                      pl.BlockSpec((1, 1, ff2), by_expert),
                      pl.BlockSpec((1, ff, D), by_expert),
                      pl.BlockSpec((1, 1, D), by_expert)],
            out_specs=pl.BlockSpec((tmb, dp), lambda j, blk, nb: (j, 0)),
            scratch_shapes=[pltpu.VMEM((D, ff2), BF16), pltpu.VMEM((ff, D), BF16)]),
        compiler_params=_params(1),
        name="moe_experts",
    )(blk_e, nblk, xs, w_gate_up, b_gate_up.reshape(E, 1, ff2), w_down, b_down.reshape(E, 1, D))


def _combine_kernel(slot_ref, ys_ref, w_ref, x1_ref, gt_ref, fg_ref, o_ref, buf, sem, *, tt, nt):
    step = pl.program_id(0) * nt + pl.program_id(1)
    nsteps = pl.num_programs(0) * nt

    def row_copy(s, t, k):
        idx = slot_ref[(s * tt + t) * TOP_K + k]
        b = s & 1
        return pltpu.make_async_copy(ys_ref.at[pl.ds(idx, 1)], buf.at[b, k, pl.ds(t, 1)], sem.at[b])

    def issue(s):
        def body(t, carry):
            for k in range(TOP_K):
                row_copy(s, t, k).start(priority=k % 2)
            return carry
        lax.fori_loop(0, tt, body, 0)

    @pl.when(step == 0)
    def _():
        issue(step)

    @pl.when(step + 1 < nsteps)
    def _():
        issue(step + 1)

    def wait(t, carry):
        for k in range(TOP_K):
            row_copy(step, t, k).wait()
        return carry

    lax.fori_loop(0, tt, wait, 0)

    cur = step & 1
    w = w_ref[0]
    moe = _unpack_bf16_pairs(buf[cur, 0]) * w[:, 0:1]
    for k in range(1, TOP_K):
        moe = moe + _unpack_bf16_pairs(buf[cur, k]) * w[:, k:k + 1]
    x2 = x1_ref[0] + gt_ref[0] * moe
    y = x2 * lax.rsqrt(jnp.mean(x2 * x2, axis=-1, keepdims=True) + EPS)
    o_ref[0] = y * fg_ref[...]


def _combine(slots_flat, ys, w_pad, x1, gt2, final_g, tt):
    B, T, D = x1.shape
    nt = T // tt
    row = lambda b, i, slots: (b, i, 0)
    return pl.pallas_call(
        functools.partial(_combine_kernel, tt=tt, nt=nt),
        out_shape=jax.ShapeDtypeStruct((B, T, D), F32),
        grid_spec=pltpu.PrefetchScalarGridSpec(
            num_scalar_prefetch=1,
            grid=(B, nt),
            in_specs=[pl.BlockSpec(memory_space=pl.ANY),
                      pl.BlockSpec((1, tt, LANES), row),
                      pl.BlockSpec((1, tt, D), row),
                      pl.BlockSpec((1, 1, D), lambda b, i, slots: (b, 0, 0)),
                      pl.BlockSpec((1, D), lambda b, i, slots: (0, 0))],
            out_specs=pl.BlockSpec((1, tt, D), row),
            scratch_shapes=[pltpu.VMEM((2, TOP_K, tt, ys.shape[1]), ys.dtype),
                            pltpu.SemaphoreType.DMA((2,))]),
        compiler_params=_params(2),
        name="moe_combine",
    )(slots_flat, ys, w_pad, x1, gt2, final_g.reshape(1, -1))


def _layer(x, c, w_ada, b_ada, norm1_g, w_in, w_gate_lr, b_gate, gla_norm_g, w_branch_a,
           w_branch_b, w_out, norm2_g, router_w, router_b, w_gate_up, b_gate_up, w_down, b_down,
           final_g):
    B, T, D = x.shape
    N = B * T
    n_exp = router_w.shape[1]
    hk = w_gate_lr.shape[1]
    hv = w_branch_b.shape[0]
    ni = IDX_HEADS * IDX_DIM

    c8 = jnp.zeros((8, D), F32).at[:B].set(c)
    mod = _adaln(c8, w_ada, b_ada)[:B]
    sh1, sc1, gt1, sh2, sc2, gt2 = [m.reshape(B, 1, D) for m in jnp.split(mod, 6, axis=-1)]

    sizes = (D, D, D, ni, IDX_DIM, IDX_HEADS, hk, hk, hv, hv, GATE_RANK, D, D)
    offs = [0]
    for s in sizes:
        offs.append(offs[-1] + s)
    seg = lambda j: w_in[:, offs[j]:offs[j + 1]]
    w_qi = jnp.pad(seg(3).reshape(D, IDX_HEADS, IDX_DIM),
                   ((0, 0), (0, 0), (0, LANES - IDX_DIM))).reshape(D, IDX_HEADS * LANES)
    groups = [seg(0), seg(1), w_qi, seg(6), seg(7), seg(8), seg(9), seg(11), seg(12)]
    widths = tuple(g.shape[1] for g in groups)
    w_main = jnp.concatenate(groups, axis=1).astype(BF16)
    w_misc = jnp.concatenate(
        [seg(4), seg(5), seg(10), jnp.zeros((D, MISC_W - IDX_DIM - IDX_HEADS - GATE_RANK), F32)],
        axis=1)
    qa, ka, qi, qb, kb, vb, gb, ga, gbm, va_t, misc, misc_t = _inproj(
        x, norm1_g.reshape(1, D), sc1, sh1, w_main, seg(2).T.astype(BF16), w_misc, widths,
        min(256, T))

    ya = _dsa(qa, ka, va_t, qi, misc, misc_t, min(512, T))

    wg_pad = jnp.zeros((MISC_W, hk), F32).at[LR_OFF:LR_OFF + GATE_RANK].set(w_gate_lr)
    yb = _gla(qb, kb, vb, gb, misc, wg_pad, b_gate, gla_norm_g, min(256, T))

    rw_pad = jnp.zeros((D, LANES), F32).at[:, :n_exp].set(router_w)
    rb_pad = jnp.zeros((1, LANES), F32).at[0, :n_exp].set(router_b)
    x1, h2, e_pad, w_pad, cnt = _merge(
        ya, yb, ga, gbm, x, gt1, sc2, sh2, norm2_g.reshape(1, D),
        w_branch_a.astype(BF16), w_branch_b.astype(BF16), w_out.astype(BF16),
        rw_pad, rb_pad, n_exp, min(512, T))

    tmb = 256
    nb = -(-N * TOP_K // tmb) + n_exp
    nbp = -(-nb // 8) * 8
    slots, blk_e, nblk = _plan(e_pad.reshape(N, LANES), cnt, n_exp, tmb, nbp, min(512, N))

    tt_d = min(256, T)
    slots_flat = slots[:, :TOP_K].reshape(-1)
    xs = _dispatch(slots_flat, h2.reshape(N, D // 2), jnp.zeros((nb * tmb, D // 2), I32), tt_d)
    ys = _experts(blk_e[:nb, 0], nblk[0, :1], xs, w_gate_up, b_gate_up, w_down, b_down, tmb)
    tt_c = min(256, T)
    return _combine(slots_flat, ys, w_pad, x1, gt2, final_g, tt_c)


def kernel(x, c, w_ada, b_ada, norm1_g, w_in, w_gate_lr, b_gate, gla_norm_g, w_branch_a, w_branch_b,
           w_out, norm2_g, router_w, router_b, w_gate_up, b_gate_up, w_down, b_down, final_g):
    assert w_ada.shape[0] == 1, "single-layer block"
    return _layer(x, c, w_ada[0], b_ada[0], norm1_g[0], w_in[0], w_gate_lr[0], b_gate[0],
                  gla_norm_g[0], w_branch_a[0], w_branch_b[0], w_out[0], norm2_g[0], router_w[0],
                  router_b[0], w_gate_up[0], b_gate_up[0], w_down[0], b_down[0], final_g)
```

```python
import functools

import jax
import jax.numpy as jnp
from jax import lax
from jax.experimental import pallas as pl
from jax.experimental.pallas import tpu as pltpu

CHUNK = 64
Q_BLOCK = 128
DSA_HEADS = 8
IDX_HEADS = 8
IDX_DIM = 64
IDX_TOPK_MAX = 256
GLA_HEADS = 4
GATE_RANK = 16
GATE_TAU = 16.0
TOP_K = 4
SWIGLU_LIMIT = 7.0
SWIGLU_ALPHA = 1.702
EPS = 1e-6

LANES = 128
ROW_GROUP = 8
MISC_W = LANES
KI_OFF, WI_OFF, LR_OFF = 0, IDX_DIM, IDX_DIM + IDX_HEADS
NEG_SCORE = -3.0e38
NEG_BIAS = -1.0e30
VMEM_LIMIT = 56 * 1024 * 1024

F32 = jnp.float32
BF16 = jnp.bfloat16
I32 = jnp.int32
HI = lax.Precision.HIGHEST


def _params(n_axes, vmem=VMEM_LIMIT):
    return pltpu.CompilerParams(dimension_semantics=("arbitrary",) * n_axes,
                                vmem_limit_bytes=vmem)


def _log2(n):
    b = n.bit_length() - 1
    assert (1 << b) == n, n
    return b


def _adaln_kernel(c_ref, w_ref, b_ref, o_ref):
    c = c_ref[...]
    s = c * jax.nn.sigmoid(c)
    o_ref[...] = jnp.dot(s, w_ref[...], precision=HI, preferred_element_type=F32) + b_ref[...]


def _adaln(c8, w_ada, b_ada):
    D = c8.shape[1]
    n = w_ada.shape[1] // D
    return pl.pallas_call(
        _adaln_kernel,
        out_shape=jax.ShapeDtypeStruct((c8.shape[0], n * D), F32),
        grid=(n,),
        in_specs=[pl.BlockSpec((c8.shape[0], D), lambda j: (0, 0)),
                  pl.BlockSpec((D, D), lambda j: (0, j)),
                  pl.BlockSpec((1, D), lambda j: (0, j))],
        out_specs=pl.BlockSpec((c8.shape[0], D), lambda j: (0, j)),
        compiler_params=_params(1),
        name="adaln",
    )(c8, w_ada, b_ada.reshape(1, -1))


def _rms_mod(x, g, sc, sh):
    y = x * lax.rsqrt(jnp.mean(x * x, axis=-1, keepdims=True) + EPS)
    return (y * g) * (1.0 + sc) + sh


_CONTRACT_LAST = (((1,), (1,)), ((), ()))
_CONTRACT_FIRST = (((0,), (0,)), ((), ()))


def _inproj_kernel(x_ref, g_ref, sc_ref, sh_ref, w_ref, wt_ref, wm_ref, wmt_ref, *out_refs,
                   widths, t_heights):
    n_tok = len(widths)
    h = _rms_mod(x_ref[0], g_ref[...], sc_ref[0], sh_ref[0])
    hb = h.astype(BF16)
    ht = h.T
    htb = ht.astype(BF16)
    h_lo = (h - hb.astype(F32)).astype(BF16)
    ht_lo = (ht - htb.astype(F32)).astype(BF16)
    pt = jnp.dot(wmt_ref[...], htb, preferred_element_type=F32)
    out_refs[-1][0] = (pt[:MISC_W] + pt[MISC_W:]
                       + jnp.dot(wmt_ref[:MISC_W, :], ht_lo, preferred_element_type=F32))
    pm = jnp.dot(hb, wm_ref[...], preferred_element_type=F32)
    out_refs[-2][0] = (pm[:, :MISC_W] + pm[:, MISC_W:]
                       + jnp.dot(h_lo, wm_ref[:, :MISC_W], preferred_element_type=F32))
    off = 0
    for o_ref, w in zip(out_refs[:n_tok], widths):
        o_ref[0] = jnp.dot(hb, w_ref[:, off:off + w], preferred_element_type=F32).astype(BF16)
        off += w
    off = 0
    for o_ref, hgt in zip(out_refs[n_tok:-2], t_heights):
        val = jnp.dot(wt_ref[off:off + hgt, :], htb, preferred_element_type=F32).astype(BF16)
        if len(o_ref.shape) == 4:
            o_ref[0, 0] = val
        else:
            o_ref[0] = val
        off += hgt


def _inproj(x, g, sc, sh, w_main, w_t, w_misc, widths, t_heights, tm):
    B, T, D = x.shape
    row = lambda b, i: (b, i, 0)
    col = lambda b, i: (b, 0, i)
    per_b = lambda b, i: (b, 0, 0)
    const = lambda b, i: (0, 0)
    resident = functools.partial(pl.BlockSpec, index_map=const, pipeline_mode=pl.Buffered(1))
    out_shape = [jax.ShapeDtypeStruct((B, T, w), BF16) for w in widths]
    out_specs = [pl.BlockSpec((1, tm, w), row) for w in widths]
    out_shape += [jax.ShapeDtypeStruct((B, hgt, T), BF16) for hgt in t_heights[:-1]]
    out_specs += [pl.BlockSpec((1, hgt, tm), col) for hgt in t_heights[:-1]]
    out_shape += [jax.ShapeDtypeStruct((B, T // tm, t_heights[-1], tm), BF16),
                  jax.ShapeDtypeStruct((B, T, MISC_W), F32),
                  jax.ShapeDtypeStruct((B, MISC_W, T), F32)]
    out_specs += [pl.BlockSpec((1, 1, t_heights[-1], tm), lambda b, i: (b, i, 0, 0)),
                  pl.BlockSpec((1, tm, MISC_W), row),
                  pl.BlockSpec((1, MISC_W, tm), col)]
    return pl.pallas_call(
        functools.partial(_inproj_kernel, widths=widths, t_heights=t_heights),
        out_shape=out_shape,
        grid=(B, T // tm),
        in_specs=[pl.BlockSpec((1, tm, D), row),
                  pl.BlockSpec((1, D), const),
                  pl.BlockSpec((1, 1, D), per_b),
                  pl.BlockSpec((1, 1, D), per_b),
                  resident(w_main.shape), resident(w_t.shape),
                  resident((D, 2 * MISC_W)), resident((2 * MISC_W, D))],
        out_specs=out_specs,
        compiler_params=_params(2),
        name="inproj",
    )(x, g, sc, sh, w_main, w_t, w_misc, w_misc.T)


def _split_bf16(w):
    hi = w.astype(BF16)
    return jnp.concatenate([hi, (w - hi.astype(F32)).astype(BF16)], axis=-1)


def _pair_loop(n, body, init):
    def two(j, carry):
        return body(2 * j + 1, body(2 * j, carry))
    carry = lax.fori_loop(0, n >> 1, two, init)
    return lax.cond((n & 1) == 1, lambda c: body(n - 1, c), lambda c: c, carry)


def _dsa_kernel(q_ref, k_ref, vt_ref, qi_ref, mall_ref, mt_ref, o_ref, sc_ref, lg0_ref, lg1_ref,
                acc_ref, *, T, DQ, KT, KC, topk, dh):
    i = pl.program_id(1)
    lg_refs = (lg0_ref, lg1_ref)
    nt = ((i + 1) * DQ + KT - 1) // KT
    qpos = i * DQ + lax.broadcasted_iota(I32, (1, DQ), 1)
    lim = ((qpos >> _log2(CHUNK)) + 1) << _log2(CHUNK)
    krow = lax.broadcasted_iota(I32, (KT, DQ), 0)

    def tile(kt):
        return pl.ds(pl.multiple_of(kt * KT, KT), KT)

    def half_tile(kt):
        return pl.ds(pl.multiple_of(kt * (KT // 2), KT // 2), KT // 2)

    def trunc_bf16(x):
        bits = lax.bitcast_convert_type(x, I32) & jnp.int32(-65536)
        return lax.bitcast_convert_type(bits, F32).astype(BF16)

    wi_t = mt_ref[0, WI_OFF:WI_OFF + IDX_HEADS, :] * ((IDX_HEADS ** -0.5) * (IDX_DIM ** -0.5))

    def score_tile(kt, carry):
        ki_t = mall_ref[0, tile(kt), :].astype(BF16)
        s = jnp.zeros((KT, DQ), F32)
        for h in range(IDX_HEADS):
            x = jnp.dot(ki_t, qi_ref[0, h * LANES:(h + 1) * LANES, :], preferred_element_type=F32)
            s = s + jnp.maximum(x, 0.0) * wi_t[h:h + 1, :]
        s = jnp.where(krow + kt * KT < lim, s, NEG_SCORE)
        sc_ref[tile(kt), :] = s
        lg1_ref[half_tile(kt), :] = pltpu.bitcast(trunc_bf16(s), F32)
        return carry

    lax.fori_loop(0, nt, score_tile, 0)

    def count(pred):
        def body(kt, acc):
            m = jnp.where(pred(sc_ref[tile(kt), :], kt * KT), 1.0, 0.0)
            return acc + jnp.sum(m, axis=0, keepdims=True)
        return lax.fori_loop(0, nt, body, jnp.zeros((1, DQ), F32))

    int_min = jnp.int32(-2 ** 31)

    def key_to_float(u):
        bits = jnp.where(u < 0, u ^ int_min, ~u)
        return lax.bitcast_convert_type(bits, F32)

    one_b, zero_b = jnp.ones((), BF16), jnp.zeros((), BF16)
    rows_b = 16

    def count_hi(tb):
        def body(kt, acc):
            hi = pltpu.bitcast(lg1_ref[half_tile(kt), :], BF16)
            m = jnp.where(hi >= tb, one_b, zero_b)
            for j in range(KT // rows_b):
                acc = acc + m[j * rows_b:(j + 1) * rows_b, :]
            return acc
        acc = lax.fori_loop(0, nt, body, jnp.zeros((rows_b, DQ), BF16))
        return jnp.sum(acc.astype(F32), axis=0, keepdims=True)

    def hi_body(it, carry):
        u, c_ge = carry
        trial = u | (jnp.int32(1) << (31 - it))
        cnt = count_hi(trunc_bf16(key_to_float(trial)))
        ok = cnt >= topk
        return jnp.where(ok, trial, u), jnp.where(ok, cnt, c_ge)

    def lo_body(it, carry):
        u, c_ge = carry
        trial = u | (jnp.int32(1) << (15 - it))
        t = key_to_float(trial)
        cnt = count(lambda s, c0: s >= t)
        ok = cnt >= topk
        return jnp.where(ok, trial, u), jnp.where(ok, cnt, c_ge)

    carry = (jnp.zeros((1, DQ), I32), jnp.zeros((1, DQ), F32))
    carry = lax.fori_loop(0, 16, hi_body, carry)
    u, c_ge = lax.fori_loop(0, 16, lo_body, carry)
    thr = key_to_float(u)

    c_gt = count(lambda s, c0: s > thr)
    need = topk - c_gt
    excess = jnp.logical_and(c_ge - c_gt > need, thr > NEG_SCORE)

    def tie_search():
        def body(it, v):
            trial = v | (jnp.int32(1) << (_log2(T) - 1 - it))
            cnt = count(lambda s, c0: jnp.logical_and(s == thr, krow + c0 < trial))
            return jnp.where(cnt < need, trial, v)
        v = lax.fori_loop(0, _log2(T), body, jnp.zeros((1, DQ), I32))
        return jnp.where(excess, v, T)

    has_excess = jnp.max(jnp.where(excess, 1.0, 0.0)) > 0.0
    cut = lax.cond(has_excess, tie_search, lambda: jnp.full((1, DQ), T, I32))

    def bias_tile(kt, carry):
        s = sc_ref[tile(kt), :]
        key = krow + kt * KT
        sel = jnp.logical_or(s > thr, jnp.logical_and(s == thr, key <= cut))
        sel = jnp.logical_and(sel, key < lim)
        sc_ref[tile(kt), :] = jnp.where(sel, 0.0, NEG_BIAS)
        return carry

    lax.fori_loop(0, nt, bias_tile, 0)

    c_exp = (dh ** -0.5) * 1.4426950408889634

    def pass_a(h, kt, mx):
        hs = slice(h * dh, (h + 1) * dh)
        lg = jnp.dot(k_ref[0, tile(kt), hs], q_ref[0, hs, :], preferred_element_type=F32)
        s = lg + sc_ref[tile(kt), :]
        lg_refs[h % 2][tile(kt), :] = s
        return jnp.maximum(mx, jnp.max(s, axis=0, keepdims=True))

    def pass_b(h, kt, mx, den):
        hs = slice(h * dh, (h + 1) * dh)
        p = jnp.exp2((lg_refs[h % 2][tile(kt), :] - mx) * c_exp)
        pb = p.astype(BF16)
        pv = jnp.zeros((dh, DQ), F32)
        for j in range(KT // KC):
            pv = pv + jnp.dot(vt_ref[0, kt * (KT // KC) + j, hs, :], pb[j * KC:(j + 1) * KC, :],
                              preferred_element_type=F32)
        acc_ref[hs, :] += pv
        return den + jnp.sum(p, axis=0, keepdims=True)

    mx0 = jnp.full((1, DQ), NEG_BIAS, F32)
    den0 = jnp.zeros((1, DQ), F32)
    acc_ref[...] = jnp.zeros(acc_ref.shape, F32)
    mx = _pair_loop(nt, functools.partial(pass_a, 0), mx0)
    for h in range(DSA_HEADS):
        if h + 1 < DSA_HEADS:
            def both(kt, carry, h=h, mx=mx):
                den, mx_next = carry
                return pass_b(h, kt, mx, den), pass_a(h + 1, kt, mx_next)
            den, mx_next = _pair_loop(nt, both, (den0, mx0))
        else:
            den = _pair_loop(nt, lambda kt, den, h=h, mx=mx: pass_b(h, kt, mx, den), den0)
            mx_next = None
        hs = slice(h * dh, (h + 1) * dh)
        o_ref[0, :, hs] = (acc_ref[hs, :] / den).T.astype(BF16)
        mx = mx_next


def _dsa(qa_t, ka, va_t, qi_t, misc, misc_t, dq):
    B, T, D = ka.shape
    dh = D // DSA_HEADS
    KC = va_t.shape[3]
    KT = min(512, T)
    topk = min(IDX_TOPK_MAX, T // 4)
    assert KT >= topk and KT % KC == 0 and dq % CHUNK == 0
    assert T // 16 <= 256, "packed bf16 counters must stay exact"
    blk = lambda b, i: (b, i, 0)
    whole = functools.partial(pl.BlockSpec, pipeline_mode=pl.Buffered(1))
    return pl.pallas_call(
        functools.partial(_dsa_kernel, T=T, DQ=dq, KT=KT, KC=KC, topk=topk, dh=dh),
        out_shape=jax.ShapeDtypeStruct((B, T, D), BF16),
        grid=(B, T // dq),
        in_specs=[pl.BlockSpec((1, D, dq), lambda b, i: (b, 0, i)),
                  whole((1, T, D), lambda b, i: (b, 0, 0)),
                  whole((1, T // KC, D, KC), lambda b, i: (b, 0, 0, 0)),
                  pl.BlockSpec((1, IDX_HEADS * LANES, dq), lambda b, i: (b, 0, i)),
                  whole((1, T, MISC_W), lambda b, i: (b, 0, 0)),
                  pl.BlockSpec((1, MISC_W, dq), lambda b, i: (b, 0, i))],
        out_specs=pl.BlockSpec((1, dq, D), blk),
        scratch_shapes=[pltpu.VMEM((T, dq), F32),
                        pltpu.VMEM((T, dq), F32),
                        pltpu.VMEM((T, dq), F32),
                        pltpu.VMEM((D, dq), F32)],
        compiler_params=_params(2),
        name="dsa",
    )(qa_t, ka, va_t, qi_t, misc, misc_t)


def _gla_kernel(q_ref, k_ref, v_ref, gb_ref, misc_ref, wg_ref, bg_ref, gn_ref, o_ref, st_ref,
                *, tb, dk, dv):
    @pl.when(pl.program_id(1) == 0)
    def _():
        st_ref[...] = jnp.zeros(st_ref.shape, F32)

    hk = wg_ref.shape[1] // 2
    m = misc_ref[0]
    m_hi = m.astype(BF16)
    m_lo = (m - m_hi.astype(F32)).astype(BF16)
    px = jnp.dot(m_hi, wg_ref[...], preferred_element_type=F32)
    x = (px[:, :hk] + px[:, hk:]
         + jnp.dot(m_lo, wg_ref[:, :hk], preferred_element_type=F32)) + bg_ref[...]
    log_a = (jnp.minimum(x, 0.0) - jnp.log1p(jnp.exp(-jnp.abs(x)))) / GATE_TAU
    la1 = log_a.astype(BF16)
    res = log_a - la1.astype(F32)
    la2 = res.astype(BF16)
    la3 = (res - la2.astype(F32)).astype(BF16)

    r = lax.broadcasted_iota(I32, (CHUNK, CHUNK), 0)
    c = lax.broadcasted_iota(I32, (CHUNK, CHUNK), 1)
    causal = r >= c
    tril = jnp.where(causal, 1.0, 0.0).astype(BF16)
    gn = gn_ref[...]
    contract_last = (((1,), (1,)), ((), ()))
    contract_first = (((0,), (0,)), ((), ()))

    for ci in range(tb // CHUNK):
        rs = slice(ci * CHUNK, (ci + 1) * CHUNK)
        g_all = (jnp.dot(tril, la1[rs], preferred_element_type=F32)
                 + jnp.dot(tril, la2[rs], preferred_element_type=F32)
                 + jnp.dot(tril, la3[rs], preferred_element_type=F32))
        for h in range(GLA_HEADS):
            ks = slice(h * dk, (h + 1) * dk)
            vs = slice(h * dv, (h + 1) * dv)
            g = g_all[:, ks]
            g_last = g[CHUNK - 1:CHUNK, :]
            qe = (q_ref[0, rs, ks].astype(F32) * (dk ** -0.5)) * jnp.exp(g)
            kh = k_ref[0, rs, ks].astype(F32)
            vh = v_ref[0, rs, vs]
            a = lax.dot_general(qe.astype(BF16), (kh * jnp.exp(-g)).astype(BF16), contract_last,
                                preferred_element_type=F32)
            a = jnp.where(causal, a, 0.0)
            st = st_ref[h]
            o = (jnp.dot(a.astype(BF16), vh, preferred_element_type=F32)
                 + lax.dot_general(qe.astype(BF16), st.astype(BF16), contract_last,
                                   preferred_element_type=F32))
            upd = lax.dot_general(vh, (kh * jnp.exp(g_last - g)).astype(BF16), contract_first,
                                  preferred_element_type=F32)
            st_ref[h] = st * jnp.exp(g_last) + upd
            y = o * lax.rsqrt(jnp.mean(o * o, axis=-1, keepdims=True) + EPS) * gn
            gate = gb_ref[0, rs, vs].astype(F32)
            o_ref[0, rs, vs] = (y * (gate * jax.nn.sigmoid(gate))).astype(BF16)


def _gla(qb, kb, vb, gb, misc, wg_pad, b_gate, gla_norm_g, tb):
    B, T, HK = qb.shape
    HV = vb.shape[2]
    dk, dv = HK // GLA_HEADS, HV // GLA_HEADS
    blk = lambda b, i: (b, i, 0)
    const = lambda b, i: (0, 0)
    return pl.pallas_call(
        functools.partial(_gla_kernel, tb=tb, dk=dk, dv=dv),
        out_shape=jax.ShapeDtypeStruct((B, T, HV), BF16),
        grid=(B, T // tb),
        in_specs=[pl.BlockSpec((1, tb, HK), blk),
                  pl.BlockSpec((1, tb, HK), blk),
                  pl.BlockSpec((1, tb, HV), blk),
                  pl.BlockSpec((1, tb, HV), blk),
                  pl.BlockSpec((1, tb, MISC_W), blk),
                  pl.BlockSpec((MISC_W, 2 * HK), const),
                  pl.BlockSpec((1, HK), const),
                  pl.BlockSpec((1, dv), const)],
        out_specs=pl.BlockSpec((1, tb, HV), blk),
        scratch_shapes=[pltpu.VMEM((GLA_HEADS, dv, dk), F32)],
        compiler_params=_params(2),
        name="gla",
    )(qb, kb, vb, gb, misc, wg_pad, b_gate.reshape(1, -1), gla_norm_g.reshape(1, -1))


def _pack_bf16_pairs(x):
    half = x.shape[1] // 2
    bits = lax.bitcast_convert_type(x.astype(BF16).astype(F32), I32)
    return (bits[:, half:] & jnp.int32(-65536)) | lax.shift_right_logical(bits[:, :half], 16)


def _unpack_bf16_pairs(w):
    lo = lax.bitcast_convert_type(w << 16, F32)
    hi = lax.bitcast_convert_type(w & jnp.int32(-65536), F32)
    return jnp.concatenate([lo, hi], axis=1)


def _merge_kernel(ya_ref, yb_ref, ga_ref, gbm_ref, x_ref, gt_ref, sc_ref, sh_ref, g2_ref,
                  wa_ref, wb_ref, wo_ref, rw_ref, rb_ref,
                  x1_ref, h2_ref, e_ref, w_ref, cnt_ref, *, n_exp):
    pa = jnp.dot(ya_ref[0], wa_ref[...], preferred_element_type=F32)
    pb = jnp.dot(yb_ref[0], wb_ref[...], preferred_element_type=F32)
    merged = (jax.nn.sigmoid(ga_ref[0].astype(F32)) * pa
              + jax.nn.sigmoid(gbm_ref[0].astype(F32)) * pb)
    mo = jnp.dot(merged.astype(BF16), wo_ref[...], preferred_element_type=F32)
    x1 = x_ref[0] + gt_ref[0] * mo
    x1_ref[0] = x1
    h2 = _rms_mod(x1, g2_ref[...], sc_ref[0], sh_ref[0])
    h2_ref[0] = _pack_bf16_pairs(h2)

    h_hi = h2.astype(BF16)
    h_lo = (h2 - h_hi.astype(F32)).astype(BF16)
    part = jnp.dot(h_hi, rw_ref[...], preferred_element_type=F32)
    logits = (part[:, :LANES] + part[:, LANES:]
              + jnp.dot(h_lo, rw_ref[:, :LANES], preferred_element_type=F32)) + rb_ref[...]
    tm = logits.shape[0]
    lane = lax.broadcasted_iota(I32, (tm, LANES), 1)
    logits = jnp.where(lane < n_exp, logits, -jnp.inf)
    top_v, top_e = [], []
    for _ in range(TOP_K):
        mx = jnp.max(logits, axis=1, keepdims=True)
        idx = jnp.min(jnp.where(logits == mx, lane, LANES), axis=1, keepdims=True)
        logits = jnp.where(lane == idx, -jnp.inf, logits)
        top_v.append(mx)
        top_e.append(idx)
    ex = [jnp.exp(v - top_v[0]) for v in top_v]
    den = ex[0]
    for t in ex[1:]:
        den = den + t
    e_out = jnp.zeros((tm, LANES), I32)
    w_out = jnp.zeros((tm, LANES), F32)
    hot = jnp.zeros((tm, LANES), F32)
    for k in range(TOP_K):
        e_out = jnp.where(lane == k, top_e[k], e_out)
        w_out = jnp.where(lane == k, ex[k] / den, w_out)
        hot = hot + jnp.where(lane == top_e[k], 1.0, 0.0)
    e_ref[0] = e_out
    w_ref[0] = w_out

    @pl.when(jnp.logical_and(pl.program_id(0) == 0, pl.program_id(1) == 0))
    def _():
        cnt_ref[...] = jnp.zeros(cnt_ref.shape, F32)

    cnt_ref[0:1, :] += jnp.sum(hot, axis=0, keepdims=True)


def _merge(ya, yb, ga, gbm, x, gt1, sc2, sh2, g2, wa, wb, wo, rw_pad, rb_pad, n_exp, tm):
    B, T, D = x.shape
    row = lambda b, i: (b, i, 0)
    per_b = lambda b, i: (b, 0, 0)
    const = lambda b, i: (0, 0)
    resident = functools.partial(pl.BlockSpec, index_map=const, pipeline_mode=pl.Buffered(1))
    return pl.pallas_call(
        functools.partial(_merge_kernel, n_exp=n_exp),
        out_shape=[jax.ShapeDtypeStruct((B, T, D), F32),
                   jax.ShapeDtypeStruct((B, T, D // 2), I32),
                   jax.ShapeDtypeStruct((B, T, LANES), I32),
                   jax.ShapeDtypeStruct((B, T, LANES), F32),
                   jax.ShapeDtypeStruct((8, LANES), F32)],
        grid=(B, T // tm),
        in_specs=[pl.BlockSpec((1, tm, D), row),
                  pl.BlockSpec((1, tm, D), row),
                  pl.BlockSpec((1, tm, D), row),
                  pl.BlockSpec((1, tm, D), row),
                  pl.BlockSpec((1, tm, D), row),
                  pl.BlockSpec((1, 1, D), per_b),
                  pl.BlockSpec((1, 1, D), per_b),
                  pl.BlockSpec((1, 1, D), per_b),
                  pl.BlockSpec((1, D), const),
                  resident((D, D)), resident((D, D)), resident((D, D)),
                  resident((D, 2 * LANES)), pl.BlockSpec((1, LANES), const)],
        out_specs=[pl.BlockSpec((1, tm, D), row),
                   pl.BlockSpec((1, tm, D // 2), row),
                   pl.BlockSpec((1, tm, LANES), row),
                   pl.BlockSpec((1, tm, LANES), row),
                   pl.BlockSpec((8, LANES), const)],
        compiler_params=_params(2),
        name="merge",
    )(ya, yb, ga, gbm, x, gt1, sc2, sh2, g2, wa, wb, wo, rw_pad, rb_pad)


def _plan_kernel(e_ref, cnt_ref, slot_ref, blk_ref, nblk_ref, base_ref, *, n_exp, tmb, nbp):
    tt = e_ref.shape[0]
    lane1 = lax.broadcasted_iota(I32, (1, LANES), 1)

    @pl.when(pl.program_id(0) == 0)
    def _():
        cnt = cnt_ref[0:1, :]
        padded = jnp.where(lane1 < n_exp, jnp.ceil(cnt / tmb) * tmb, 0.0)
        r = lax.broadcasted_iota(I32, (LANES, LANES), 0)
        c = lax.broadcasted_iota(I32, (LANES, LANES), 1)
        upper = jnp.where(r < c, 1.0, 0.0)
        start = jnp.dot(jnp.broadcast_to(padded, (8, LANES)), upper, precision=HI,
                        preferred_element_type=F32)[0:1, :]
        base_ref[...] = start
        end = start + padded
        jrow = lax.broadcasted_iota(I32, (nbp, LANES), 0).astype(F32) * tmb
        owner = jnp.sum(jnp.where(jnp.logical_and(end <= jrow, lane1 < n_exp), 1.0, 0.0),
                        axis=1, keepdims=True)
        blk_ref[...] = jnp.broadcast_to(jnp.minimum(owner, n_exp - 1.0), (nbp, LANES)).astype(I32)
        total = jnp.sum(padded, axis=1, keepdims=True)
        nblk_ref[...] = jnp.broadcast_to(total / tmb, (8, LANES)).astype(I32)

    e = e_ref[...]
    lane = lax.broadcasted_iota(I32, (tt, LANES), 1)
    hot = jnp.zeros((tt, LANES), F32)
    for k in range(TOP_K):
        hot = hot + jnp.where(lane == e[:, k:k + 1], 1.0, 0.0)
    r = lax.broadcasted_iota(I32, (tt, tt), 0)
    c = lax.broadcasted_iota(I32, (tt, tt), 1)
    lower = jnp.where(r > c, 1.0, 0.0).astype(BF16)
    rank = jnp.dot(lower, hot.astype(BF16), preferred_element_type=F32)
    pos = base_ref[...] + rank
    out = jnp.zeros((tt, LANES), F32)
    for k in range(TOP_K):
        sk = jnp.sum(jnp.where(lane == e[:, k:k + 1], pos, 0.0), axis=1, keepdims=True)
        out = jnp.where(lane == k, sk, out)
    slot_ref[...] = out.astype(I32)
    base_ref[...] += jnp.sum(hot, axis=0, keepdims=True)


def _plan(e_pad, cnt, n_exp, tmb, nbp, tt):
    N = e_pad.shape[0]
    return pl.pallas_call(
        functools.partial(_plan_kernel, n_exp=n_exp, tmb=tmb, nbp=nbp),
        out_shape=[jax.ShapeDtypeStruct((N, LANES), I32),
                   jax.ShapeDtypeStruct((nbp, LANES), I32),
                   jax.ShapeDtypeStruct((8, LANES), I32)],
        grid=(N // tt,),
        in_specs=[pl.BlockSpec((tt, LANES), lambda i: (i, 0)),
                  pl.BlockSpec((8, LANES), lambda i: (0, 0))],
        out_specs=[pl.BlockSpec((tt, LANES), lambda i: (i, 0)),
                   pl.BlockSpec((nbp, LANES), lambda i: (0, 0)),
                   pl.BlockSpec((8, LANES), lambda i: (0, 0))],
        scratch_shapes=[pltpu.VMEM((1, LANES), F32)],
        compiler_params=_params(1),
        name="moe_plan",
    )(e_pad, cnt)


def _dispatch_kernel(slot_ref, h_ref, xs_in_ref, xs_ref, sem, *, tt):
    del xs_in_ref

    def start(g, carry):
        t0 = pl.multiple_of(g * ROW_GROUP, ROW_GROUP)
        for j in range(ROW_GROUP):
            for k in range(TOP_K):
                s = slot_ref[(t0 + j) * TOP_K + k]
                pltpu.make_async_copy(h_ref.at[pl.ds(t0 + j, 1)], xs_ref.at[pl.ds(s, 1)],
                                      sem).start(priority=k % 2)
        return carry

    lax.fori_loop(0, tt // ROW_GROUP, start, 0)
    for k in range(TOP_K):
        pltpu.make_async_copy(h_ref, xs_ref.at[pl.ds(0, tt)], sem).wait()


def _dispatch(slots_flat, h2, xs0, tt):
    N, D = h2.shape
    return pl.pallas_call(
        functools.partial(_dispatch_kernel, tt=tt),
        out_shape=jax.ShapeDtypeStruct(xs0.shape, xs0.dtype),
        grid=(N // tt,),
        in_specs=[pl.BlockSpec((tt * TOP_K,), lambda i: (i,), memory_space=pltpu.SMEM),
                  pl.BlockSpec((tt, D), lambda i: (i, 0)),
                  pl.BlockSpec(memory_space=pl.ANY)],
        out_specs=pl.BlockSpec(memory_space=pl.ANY),
        scratch_shapes=[pltpu.SemaphoreType.DMA],
        input_output_aliases={2: 0},
        compiler_params=_params(1),
        name="moe_dispatch",
    )(slots_flat, h2, xs0)


def _experts_kernel(blk_ref, nblk_ref, xs_ref, wgu_ref, bgu_ref, wd_ref, bd_ref, ys_ref,
                    wgu_bf, wd_bf, *, ff):
    j = pl.program_id(0)
    e = blk_ref[j]
    fresh = jnp.logical_or(j == 0, e != blk_ref[jnp.maximum(j - 1, 0)])
    active = j < nblk_ref[0]

    @pl.when(jnp.logical_and(active, fresh))
    def _():
        wgu_bf[...] = wgu_ref[0].astype(BF16)
        wd_bf[...] = wd_ref[0].astype(BF16)

    @pl.when(active)
    def _():
        x = _unpack_bf16_pairs(xs_ref[...]).astype(BF16)
        hgu = jnp.dot(x, wgu_bf[...], preferred_element_type=F32) + bgu_ref[0]
        gate = jnp.minimum(hgu[:, :ff], SWIGLU_LIMIT)
        up = jnp.clip(hgu[:, ff:], -SWIGLU_LIMIT, SWIGLU_LIMIT)
        act = (up + 1.0) * (gate * jax.nn.sigmoid(SWIGLU_ALPHA * gate))
        y = jnp.dot(act.astype(BF16), wd_bf[...], preferred_element_type=F32) + bd_ref[0]
        ys_ref[...] = _pack_bf16_pairs(y)

    @pl.when(jnp.logical_not(active))
    def _():
        ys_ref[...] = jnp.zeros(ys_ref.shape, I32)


def _experts(blk_e, nblk, xs, w_gate_up, b_gate_up, w_down, b_down, tmb):
    P, dp = xs.shape
    E, D, ff2 = w_gate_up.shape
    ff = ff2 // 2
    by_expert = lambda j, blk, nb: (blk[j], 0, 0)
    return pl.pallas_call(
        functools.partial(_experts_kernel, ff=ff),
        out_shape=jax.ShapeDtypeStruct((P, dp), I32),
        grid_spec=pltpu.PrefetchScalarGridSpec(
            num_scalar_prefetch=2,
            grid=(P // tmb,),
            in_specs=[pl.BlockSpec((tmb, dp), lambda j, blk, nb: (j, 0)),
                      pl.BlockSpec((1, D, ff2), by_expert),
                      pl.BlockSpec((1, 1, ff2), by_expert),
                      pl.BlockSpec((1, ff, D), by_expert),
                      pl.BlockSpec((1, 1, D), by_expert)],
            out_specs=pl.BlockSpec((tmb, dp), lambda j, blk, nb: (j, 0)),
            scratch_shapes=[pltpu.VMEM((D, ff2), BF16), pltpu.VMEM((ff, D), BF16)]),
        compiler_params=_params(1),
        name="moe_experts",
    )(blk_e, nblk, xs, w_gate_up, b_gate_up.reshape(E, 1, ff2), w_down, b_down.reshape(E, 1, D))


def _combine_kernel(slot_ref, ys_ref, w_ref, x1_ref, gt_ref, fg_ref, o_ref, buf, sem, *, tt, nt):
    step = pl.program_id(0) * nt + pl.program_id(1)
    nsteps = pl.num_programs(0) * nt

    def issue(s):
        b = s & 1

        def body(g, carry):
            t0 = pl.multiple_of(g * ROW_GROUP, ROW_GROUP)
            for j in range(ROW_GROUP):
                for k in range(TOP_K):
                    idx = slot_ref[(s * tt + t0 + j) * TOP_K + k]
                    pltpu.make_async_copy(ys_ref.at[pl.ds(idx, 1)], buf.at[b, k, pl.ds(t0 + j, 1)],
                                          sem.at[b]).start(priority=k % 2)
            return carry
        lax.fori_loop(0, tt // ROW_GROUP, body, 0)

    @pl.when(step == 0)
    def _():
        issue(step)

    @pl.when(step + 1 < nsteps)
    def _():
        issue(step + 1)

    cur = step & 1
    for k in range(TOP_K):
        pltpu.make_async_copy(ys_ref.at[pl.ds(0, tt)], buf.at[cur, k], sem.at[cur]).wait()

    w = w_ref[0]
    moe = _unpack_bf16_pairs(buf[cur, 0]) * w[:, 0:1]
    for k in range(1, TOP_K):
        moe = moe + _unpack_bf16_pairs(buf[cur, k]) * w[:, k:k + 1]
    x2 = x1_ref[0] + gt_ref[0] * moe
    y = x2 * lax.rsqrt(jnp.mean(x2 * x2, axis=-1, keepdims=True) + EPS)
    o_ref[0] = y * fg_ref[...]


def _combine(slots_flat, ys, w_pad, x1, gt2, final_g, tt):
    B, T, D = x1.shape
    nt = T // tt
    row = lambda b, i, slots: (b, i, 0)
    return pl.pallas_call(
        functools.partial(_combine_kernel, tt=tt, nt=nt),
        out_shape=jax.ShapeDtypeStruct((B, T, D), F32),
        grid_spec=pltpu.PrefetchScalarGridSpec(
            num_scalar_prefetch=1,
            grid=(B, nt),
            in_specs=[pl.BlockSpec(memory_space=pl.ANY),
                      pl.BlockSpec((1, tt, LANES), row),
                      pl.BlockSpec((1, tt, D), row),
                      pl.BlockSpec((1, 1, D), lambda b, i, slots: (b, 0, 0)),
                      pl.BlockSpec((1, D), lambda b, i, slots: (0, 0))],
            out_specs=pl.BlockSpec((1, tt, D), row),
            scratch_shapes=[pltpu.VMEM((2, TOP_K, tt, ys.shape[1]), ys.dtype),
                            pltpu.SemaphoreType.DMA((2,))]),
        compiler_params=_params(2),
        name="moe_combine",
    )(slots_flat, ys, w_pad, x1, gt2, final_g.reshape(1, -1))


def _layer(x, c, w_ada, b_ada, norm1_g, w_in, w_gate_lr, b_gate, gla_norm_g, w_branch_a,
           w_branch_b, w_out, norm2_g, router_w, router_b, w_gate_up, b_gate_up, w_down, b_down,
           final_g):
    B, T, D = x.shape
    N = B * T
    n_exp = router_w.shape[1]
    hk = w_gate_lr.shape[1]
    hv = w_branch_b.shape[0]
    ni = IDX_HEADS * IDX_DIM

    c8 = jnp.zeros((8, D), F32).at[:B].set(c)
    mod = _adaln(c8, w_ada, b_ada)[:B]
    sh1, sc1, gt1, sh2, sc2, gt2 = [m.reshape(B, 1, D) for m in jnp.split(mod, 6, axis=-1)]

    sizes = (D, D, D, ni, IDX_DIM, IDX_HEADS, hk, hk, hv, hv, GATE_RANK, D, D)
    offs = [0]
    for s in sizes:
        offs.append(offs[-1] + s)
    seg = lambda j: w_in[:, offs[j]:offs[j + 1]]
    w_qi = jnp.pad(seg(3).reshape(D, IDX_HEADS, IDX_DIM),
                   ((0, 0), (0, 0), (0, LANES - IDX_DIM))).reshape(D, IDX_HEADS * LANES)
    groups = [seg(1), seg(6), seg(7), seg(8), seg(9), seg(11), seg(12)]
    widths = tuple(g.shape[1] for g in groups)
    w_main = jnp.concatenate(groups, axis=1).astype(BF16)
    t_groups = [seg(0), w_qi, seg(2)]
    t_heights = tuple(g.shape[1] for g in t_groups)
    w_t = jnp.concatenate(t_groups, axis=1).T.astype(BF16)
    w_misc = jnp.concatenate(
        [seg(4), seg(5), seg(10), jnp.zeros((D, MISC_W - IDX_DIM - IDX_HEADS - GATE_RANK), F32)],
        axis=1)
    ka, qb, kb, vb, gb, ga, gbm, qa_t, qi_t, va_t, misc, misc_t = _inproj(
        x, norm1_g.reshape(1, D), sc1, sh1, w_main, w_t, _split_bf16(w_misc), widths, t_heights,
        min(256, T))

    ya = _dsa(qa_t, ka, va_t, qi_t, misc, misc_t, min(512, T))

    wg_pad = jnp.zeros((MISC_W, hk), F32).at[LR_OFF:LR_OFF + GATE_RANK].set(w_gate_lr)
    yb = _gla(qb, kb, vb, gb, misc, _split_bf16(wg_pad), b_gate, gla_norm_g, min(256, T))

    rw_pad = _split_bf16(jnp.zeros((D, LANES), F32).at[:, :n_exp].set(router_w))
    rb_pad = jnp.zeros((1, LANES), F32).at[0, :n_exp].set(router_b)
    x1, h2, e_pad, w_pad, cnt = _merge(
        ya, yb, ga, gbm, x, gt1, sc2, sh2, norm2_g.reshape(1, D),
        w_branch_a.astype(BF16), w_branch_b.astype(BF16), w_out.astype(BF16),
        rw_pad, rb_pad, n_exp, min(512, T))

    tmb = 256
    nb = -(-N * TOP_K // tmb) + n_exp
    nbp = -(-nb // 8) * 8
    slots, blk_e, nblk = _plan(e_pad.reshape(N, LANES), cnt, n_exp, tmb, nbp, min(512, N))

    tt_d = min(256, T)
    slots_flat = slots[:, :TOP_K].reshape(-1)
    xs = _dispatch(slots_flat, h2.reshape(N, D // 2), jnp.zeros((nb * tmb, D // 2), I32), tt_d)
    ys = _experts(blk_e[:nb, 0], nblk[0, :1], xs, w_gate_up, b_gate_up, w_down, b_down, tmb)
    tt_c = min(256, T)
    return _combine(slots_flat, ys, w_pad, x1, gt2, final_g, tt_c)


def kernel(x, c, w_ada, b_ada, norm1_g, w_in, w_gate_lr, b_gate, gla_norm_g, w_branch_a, w_branch_b,
           w_out, norm2_g, router_w, router_b, w_gate_up, b_gate_up, w_down, b_down, final_g):
    assert w_ada.shape[0] == 1, "single-layer block"
    return _layer(x, c, w_ada[0], b_ada[0], norm1_g[0], w_in[0], w_gate_lr[0], b_gate[0],
                  gla_norm_g[0], w_branch_a[0], w_branch_b[0], w_out[0], norm2_g[0], router_w[0],
                  router_b[0], w_gate_up[0], b_gate_up[0], w_down[0], b_down[0], final_g)
```

```python
import functools

import jax
import jax.numpy as jnp
from jax import lax
from jax.experimental import pallas as pl
from jax.experimental.pallas import tpu as pltpu

CHUNK = 64
Q_BLOCK = 128
DSA_HEADS = 8
IDX_HEADS = 8
IDX_DIM = 64
IDX_TOPK_MAX = 256
GLA_HEADS = 4
GATE_RANK = 16
GATE_TAU = 16.0
TOP_K = 4
SWIGLU_LIMIT = 7.0
SWIGLU_ALPHA = 1.702
EPS = 1e-6

LANES = 128
ROW_GROUP = 8
MISC_W = LANES
KI_OFF, WI_OFF, LR_OFF = 0, IDX_DIM, IDX_DIM + IDX_HEADS
NEG_SCORE = -3.0e38
NEG_BIAS = -1.0e30
VMEM_LIMIT = 56 * 1024 * 1024

F32 = jnp.float32
BF16 = jnp.bfloat16
I32 = jnp.int32
HI = lax.Precision.HIGHEST


def _params(n_axes, vmem=VMEM_LIMIT):
    return pltpu.CompilerParams(dimension_semantics=("arbitrary",) * n_axes,
                                vmem_limit_bytes=vmem)


def _log2(n):
    b = n.bit_length() - 1
    assert (1 << b) == n, n
    return b


def _adaln_kernel(c_ref, w_ref, b_ref, o_ref):
    c = c_ref[...]
    s = c * jax.nn.sigmoid(c)
    o_ref[...] = jnp.dot(s, w_ref[...], precision=HI, preferred_element_type=F32) + b_ref[...]


def _adaln(c8, w_ada, b_ada):
    D = c8.shape[1]
    n = w_ada.shape[1] // D
    return pl.pallas_call(
        _adaln_kernel,
        out_shape=jax.ShapeDtypeStruct((c8.shape[0], n * D), F32),
        grid=(n,),
        in_specs=[pl.BlockSpec((c8.shape[0], D), lambda j: (0, 0)),
                  pl.BlockSpec((D, D), lambda j: (0, j)),
                  pl.BlockSpec((1, D), lambda j: (0, j))],
        out_specs=pl.BlockSpec((c8.shape[0], D), lambda j: (0, j)),
        compiler_params=_params(1),
        name="adaln",
    )(c8, w_ada, b_ada.reshape(1, -1))


def _rms_mod(x, g, sc, sh):
    y = x * lax.rsqrt(jnp.mean(x * x, axis=-1, keepdims=True) + EPS)
    return (y * g) * (1.0 + sc) + sh


_CONTRACT_LAST = (((1,), (1,)), ((), ()))
_CONTRACT_FIRST = (((0,), (0,)), ((), ()))


def _inproj_kernel(x_ref, g_ref, sc_ref, sh_ref, w_ref, wt_ref, wm_ref, wmt_ref, *out_refs,
                   widths, t_heights):
    n_tok = len(widths)
    h = _rms_mod(x_ref[0], g_ref[...], sc_ref[0], sh_ref[0])
    hb = h.astype(BF16)
    ht = h.T
    htb = ht.astype(BF16)
    h_lo = (h - hb.astype(F32)).astype(BF16)
    ht_lo = (ht - htb.astype(F32)).astype(BF16)
    pt = jnp.dot(wmt_ref[...], htb, preferred_element_type=F32)
    out_refs[-1][0] = (pt[:MISC_W] + pt[MISC_W:]
                       + jnp.dot(wmt_ref[:MISC_W, :], ht_lo, preferred_element_type=F32))
    pm = jnp.dot(hb, wm_ref[...], preferred_element_type=F32)
    out_refs[-2][0] = (pm[:, :MISC_W] + pm[:, MISC_W:]
                       + jnp.dot(h_lo, wm_ref[:, :MISC_W], preferred_element_type=F32))
    off = 0
    for o_ref, w in zip(out_refs[:n_tok], widths):
        o_ref[0] = jnp.dot(hb, w_ref[:, off:off + w], preferred_element_type=F32).astype(BF16)
        off += w
    off = 0
    for o_ref, hgt in zip(out_refs[n_tok:-2], t_heights):
        val = jnp.dot(wt_ref[off:off + hgt, :], htb, preferred_element_type=F32).astype(BF16)
        if len(o_ref.shape) == 4:
            o_ref[0, 0] = val
        elif o_ref.shape[1] == hgt:
            o_ref[0] = val
        else:
            for j in range(hgt // IDX_DIM):
                o_ref[0, j * LANES:j * LANES + IDX_DIM, :] = val[j * IDX_DIM:(j + 1) * IDX_DIM, :]
                o_ref[0, j * LANES + IDX_DIM:(j + 1) * LANES, :] = jnp.zeros(
                    (LANES - IDX_DIM, val.shape[1]), BF16)
        off += hgt


def _inproj(x, g, sc, sh, w_main, w_t, w_misc, widths, t_heights, t_out_heights, tm):
    B, T, D = x.shape
    row = lambda b, i: (b, i, 0)
    col = lambda b, i: (b, 0, i)
    per_b = lambda b, i: (b, 0, 0)
    const = lambda b, i: (0, 0)
    resident = functools.partial(pl.BlockSpec, index_map=const, pipeline_mode=pl.Buffered(1))
    out_shape = [jax.ShapeDtypeStruct((B, T, w), BF16) for w in widths]
    out_specs = [pl.BlockSpec((1, tm, w), row) for w in widths]
    out_shape += [jax.ShapeDtypeStruct((B, hgt, T), BF16) for hgt in t_out_heights[:-1]]
    out_specs += [pl.BlockSpec((1, hgt, tm), col) for hgt in t_out_heights[:-1]]
    out_shape += [jax.ShapeDtypeStruct((B, T // tm, t_heights[-1], tm), BF16),
                  jax.ShapeDtypeStruct((B, T, MISC_W), F32),
                  jax.ShapeDtypeStruct((B, MISC_W, T), F32)]
    out_specs += [pl.BlockSpec((1, 1, t_heights[-1], tm), lambda b, i: (b, i, 0, 0)),
                  pl.BlockSpec((1, tm, MISC_W), row),
                  pl.BlockSpec((1, MISC_W, tm), col)]
    return pl.pallas_call(
        functools.partial(_inproj_kernel, widths=widths, t_heights=t_heights),
        out_shape=out_shape,
        grid=(B, T // tm),
        in_specs=[pl.BlockSpec((1, tm, D), row),
                  pl.BlockSpec((1, D), const),
                  pl.BlockSpec((1, 1, D), per_b),
                  pl.BlockSpec((1, 1, D), per_b),
                  resident(w_main.shape), resident(w_t.shape),
                  resident((D, 2 * MISC_W)), resident((2 * MISC_W, D))],
        out_specs=out_specs,
        compiler_params=_params(2),
        name="inproj",
    )(x, g, sc, sh, w_main, w_t, w_misc, w_misc.T)


def _split_bf16(w):
    hi = w.astype(BF16)
    return jnp.concatenate([hi, (w - hi.astype(F32)).astype(BF16)], axis=-1)


def _pair_loop(n, body, init):
    def two(j, carry):
        return body(2 * j + 1, body(2 * j, carry))
    carry = lax.fori_loop(0, n >> 1, two, init)
    return lax.cond((n & 1) == 1, lambda c: body(n - 1, c), lambda c: c, carry)


def _dsa_kernel(q_ref, k_ref, vt_ref, qi_ref, mall_ref, mt_ref, o_ref, sc_ref, lg0_ref, lg1_ref,
                acc_ref, *, T, DQ, KT, KC, topk, dh):
    i = pl.program_id(1)
    lg_refs = (lg0_ref, lg1_ref)
    nt = ((i + 1) * DQ + KT - 1) // KT
    qpos = i * DQ + lax.broadcasted_iota(I32, (1, DQ), 1)
    lim = ((qpos >> _log2(CHUNK)) + 1) << _log2(CHUNK)
    krow = lax.broadcasted_iota(I32, (KT, DQ), 0)

    def tile(kt):
        return pl.ds(pl.multiple_of(kt * KT, KT), KT)

    def half_tile(kt):
        return pl.ds(pl.multiple_of(kt * (KT // 2), KT // 2), KT // 2)

    def trunc_bf16(x):
        bits = lax.bitcast_convert_type(x, I32) & jnp.int32(-65536)
        return lax.bitcast_convert_type(bits, F32).astype(BF16)

    wi_t = mt_ref[0, WI_OFF:WI_OFF + IDX_HEADS, :] * ((IDX_HEADS ** -0.5) * (IDX_DIM ** -0.5))

    def score_tile(kt, carry):
        ki_t = mall_ref[0, tile(kt), :].astype(BF16)
        s = jnp.zeros((KT, DQ), F32)
        for h in range(IDX_HEADS):
            x = jnp.dot(ki_t, qi_ref[0, h * LANES:(h + 1) * LANES, :], preferred_element_type=F32)
            s = s + jnp.maximum(x, 0.0) * wi_t[h:h + 1, :]
        s = jnp.where(krow + kt * KT < lim, s, NEG_SCORE)
        sc_ref[tile(kt), :] = s
        lg1_ref[half_tile(kt), :] = pltpu.bitcast(trunc_bf16(s), F32)
        return carry

    lax.fori_loop(0, nt, score_tile, 0)

    def count(pred):
        def body(kt, acc):
            m = jnp.where(pred(sc_ref[tile(kt), :], kt * KT), 1.0, 0.0)
            return acc + jnp.sum(m, axis=0, keepdims=True)
        return lax.fori_loop(0, nt, body, jnp.zeros((1, DQ), F32))

    int_min = jnp.int32(-2 ** 31)

    def key_to_float(u):
        bits = jnp.where(u < 0, u ^ int_min, ~u)
        return lax.bitcast_convert_type(bits, F32)

    one_b, zero_b = jnp.ones((), BF16), jnp.zeros((), BF16)
    rows_b = 16

    def count_hi(tb):
        def body(kt, acc):
            hi = pltpu.bitcast(lg1_ref[half_tile(kt), :], BF16)
            m = jnp.where(hi >= tb, one_b, zero_b)
            for j in range(KT // rows_b):
                acc = acc + m[j * rows_b:(j + 1) * rows_b, :]
            return acc
        acc = lax.fori_loop(0, nt, body, jnp.zeros((rows_b, DQ), BF16))
        return jnp.sum(acc.astype(F32), axis=0, keepdims=True)

    def hi_body(it, carry):
        u, c_ge = carry
        trial = u | (jnp.int32(1) << (31 - it))
        cnt = count_hi(trunc_bf16(key_to_float(trial)))
        ok = cnt >= topk
        return jnp.where(ok, trial, u), jnp.where(ok, cnt, c_ge)

    def lo_body(it, carry):
        u, c_ge = carry
        trial = u | (jnp.int32(1) << (15 - it))
        t = key_to_float(trial)
        cnt = count(lambda s, c0: s >= t)
        ok = cnt >= topk
        return jnp.where(ok, trial, u), jnp.where(ok, cnt, c_ge)

    carry = (jnp.zeros((1, DQ), I32), jnp.zeros((1, DQ), F32))
    carry = lax.fori_loop(0, 16, hi_body, carry)
    u, c_ge = lax.fori_loop(0, 16, lo_body, carry)
    thr = key_to_float(u)

    c_gt = count(lambda s, c0: s > thr)
    need = topk - c_gt
    excess = jnp.logical_and(c_ge - c_gt > need, thr > NEG_SCORE)
    has_excess = jnp.max(jnp.where(excess, 1.0, 0.0)) > 0.0

    def store_bias(kt, sel):
        sel = jnp.logical_and(sel, krow + kt * KT < lim)
        sc_ref[tile(kt), :] = jnp.where(sel, 0.0, NEG_BIAS)

    def bias_no_ties():
        def body(kt, carry):
            store_bias(kt, sc_ref[tile(kt), :] >= thr)
            return carry
        lax.fori_loop(0, nt, body, 0)

    def bias_with_ties():
        r = lax.broadcasted_iota(I32, (KT, KT), 0)
        c = lax.broadcasted_iota(I32, (KT, KT), 1)
        before = jnp.where(c < r, 1.0, 0.0).astype(BF16)

        def body(kt, seen):
            s = sc_ref[tile(kt), :]
            eq = jnp.where(s == thr, 1.0, 0.0)
            rank = seen + jnp.dot(before, eq.astype(BF16), preferred_element_type=F32)
            store_bias(kt, jnp.logical_or(s > thr, jnp.logical_and(s == thr, rank < need)))
            return seen + jnp.sum(eq, axis=0, keepdims=True)
        lax.fori_loop(0, nt, body, jnp.zeros((1, DQ), F32))

    lax.cond(has_excess, bias_with_ties, bias_no_ties)

    c_exp = (dh ** -0.5) * 1.4426950408889634

    def pass_a(h, kt, mx):
        hs = slice(h * dh, (h + 1) * dh)
        lg = jnp.dot(k_ref[0, tile(kt), hs], q_ref[0, hs, :], preferred_element_type=F32)
        s = lg + sc_ref[tile(kt), :]
        lg_refs[h % 2][tile(kt), :] = s
        return jnp.maximum(mx, jnp.max(s, axis=0, keepdims=True))

    def pass_b(h, kt, mx, den):
        hs = slice(h * dh, (h + 1) * dh)
        p = jnp.exp2((lg_refs[h % 2][tile(kt), :] - mx) * c_exp)
        pb = p.astype(BF16)
        pv = jnp.zeros((dh, DQ), F32)
        for j in range(KT // KC):
            pv = pv + jnp.dot(vt_ref[0, kt * (KT // KC) + j, hs, :], pb[j * KC:(j + 1) * KC, :],
                              preferred_element_type=F32)
        acc_ref[hs, :] += pv
        return den + jnp.sum(p, axis=0, keepdims=True)

    mx0 = jnp.full((1, DQ), NEG_BIAS, F32)
    den0 = jnp.zeros((1, DQ), F32)
    acc_ref[...] = jnp.zeros(acc_ref.shape, F32)
    mx = _pair_loop(nt, functools.partial(pass_a, 0), mx0)
    for h in range(DSA_HEADS):
        if h + 1 < DSA_HEADS:
            def both(kt, carry, h=h, mx=mx):
                den, mx_next = carry
                return pass_b(h, kt, mx, den), pass_a(h + 1, kt, mx_next)
            den, mx_next = _pair_loop(nt, both, (den0, mx0))
        else:
            den = _pair_loop(nt, lambda kt, den, h=h, mx=mx: pass_b(h, kt, mx, den), den0)
            mx_next = None
        hs = slice(h * dh, (h + 1) * dh)
        o_ref[0, :, hs] = (acc_ref[hs, :] / den).T.astype(BF16)
        mx = mx_next


def _dsa(qa_t, ka, va_t, qi_t, misc, misc_t, dq):
    B, T, D = ka.shape
    dh = D // DSA_HEADS
    KC = va_t.shape[3]
    KT = min(512, T)
    topk = min(IDX_TOPK_MAX, T // 4)
    assert KT >= topk and KT % KC == 0 and dq % CHUNK == 0
    assert T // 16 <= 256, "packed bf16 counters must stay exact"
    blk = lambda b, i: (b, i, 0)
    whole = functools.partial(pl.BlockSpec, pipeline_mode=pl.Buffered(1))
    return pl.pallas_call(
        functools.partial(_dsa_kernel, T=T, DQ=dq, KT=KT, KC=KC, topk=topk, dh=dh),
        out_shape=jax.ShapeDtypeStruct((B, T, D), BF16),
        grid=(B, T // dq),
        in_specs=[pl.BlockSpec((1, D, dq), lambda b, i: (b, 0, i)),
                  whole((1, T, D), lambda b, i: (b, 0, 0)),
                  whole((1, T // KC, D, KC), lambda b, i: (b, 0, 0, 0)),
                  pl.BlockSpec((1, IDX_HEADS * LANES, dq), lambda b, i: (b, 0, i)),
                  whole((1, T, MISC_W), lambda b, i: (b, 0, 0)),
                  pl.BlockSpec((1, MISC_W, dq), lambda b, i: (b, 0, i))],
        out_specs=pl.BlockSpec((1, dq, D), blk),
        scratch_shapes=[pltpu.VMEM((T, dq), F32),
                        pltpu.VMEM((T, dq), F32),
                        pltpu.VMEM((T, dq), F32),
                        pltpu.VMEM((D, dq), F32)],
        compiler_params=_params(2),
        name="dsa",
    )(qa_t, ka, va_t, qi_t, misc, misc_t)


def _gla_kernel(q_ref, k_ref, v_ref, gb_ref, misc_ref, wg_ref, bg_ref, gn_ref, o_ref, st_ref,
                *, tb, dk, dv):
    @pl.when(pl.program_id(1) == 0)
    def _():
        st_ref[...] = jnp.zeros(st_ref.shape, F32)

    hk = wg_ref.shape[1] // 2
    m = misc_ref[0]
    m_hi = m.astype(BF16)
    m_lo = (m - m_hi.astype(F32)).astype(BF16)
    px = jnp.dot(m_hi, wg_ref[...], preferred_element_type=F32)
    x = (px[:, :hk] + px[:, hk:]
         + jnp.dot(m_lo, wg_ref[:, :hk], preferred_element_type=F32)) + bg_ref[...]
    log_a = (jnp.minimum(x, 0.0) - jnp.log1p(jnp.exp(-jnp.abs(x)))) / GATE_TAU
    la1 = log_a.astype(BF16)
    res = log_a - la1.astype(F32)
    la2 = res.astype(BF16)
    la3 = (res - la2.astype(F32)).astype(BF16)

    r = lax.broadcasted_iota(I32, (CHUNK, CHUNK), 0)
    c = lax.broadcasted_iota(I32, (CHUNK, CHUNK), 1)
    causal = r >= c
    tril = jnp.where(causal, 1.0, 0.0).astype(BF16)
    gn = gn_ref[...]
    contract_last = (((1,), (1,)), ((), ()))
    contract_first = (((0,), (0,)), ((), ()))

    for ci in range(tb // CHUNK):
        rs = slice(ci * CHUNK, (ci + 1) * CHUNK)
        g_all = (jnp.dot(tril, la1[rs], preferred_element_type=F32)
                 + jnp.dot(tril, la2[rs], preferred_element_type=F32)
                 + jnp.dot(tril, la3[rs], preferred_element_type=F32))
        for h in range(GLA_HEADS):
            ks = slice(h * dk, (h + 1) * dk)
            vs = slice(h * dv, (h + 1) * dv)
            g = g_all[:, ks]
            g_last = g[CHUNK - 1:CHUNK, :]
            qe = (q_ref[0, rs, ks].astype(F32) * (dk ** -0.5)) * jnp.exp(g)
            kh = k_ref[0, rs, ks].astype(F32)
            vh = v_ref[0, rs, vs]
            a = lax.dot_general(qe.astype(BF16), (kh * jnp.exp(-g)).astype(BF16), contract_last,
                                preferred_element_type=F32)
            a = jnp.where(causal, a, 0.0)
            st = st_ref[h]
            o = (jnp.dot(a.astype(BF16), vh, preferred_element_type=F32)
                 + lax.dot_general(qe.astype(BF16), st.astype(BF16), contract_last,
                                   preferred_element_type=F32))
            upd = lax.dot_general(vh, (kh * jnp.exp(g_last - g)).astype(BF16), contract_first,
                                  preferred_element_type=F32)
            st_ref[h] = st * jnp.exp(g_last) + upd
            y = o * lax.rsqrt(jnp.mean(o * o, axis=-1, keepdims=True) + EPS) * gn
            gate = gb_ref[0, rs, vs].astype(F32)
            o_ref[0, rs, vs] = (y * (gate * jax.nn.sigmoid(gate))).astype(BF16)


def _gla(qb, kb, vb, gb, misc, wg_pad, b_gate, gla_norm_g, tb):
    B, T, HK = qb.shape
    HV = vb.shape[2]
    dk, dv = HK // GLA_HEADS, HV // GLA_HEADS
    blk = lambda b, i: (b, i, 0)
    const = lambda b, i: (0, 0)
    return pl.pallas_call(
        functools.partial(_gla_kernel, tb=tb, dk=dk, dv=dv),
        out_shape=jax.ShapeDtypeStruct((B, T, HV), BF16),
        grid=(B, T // tb),
        in_specs=[pl.BlockSpec((1, tb, HK), blk),
                  pl.BlockSpec((1, tb, HK), blk),
                  pl.BlockSpec((1, tb, HV), blk),
                  pl.BlockSpec((1, tb, HV), blk),
                  pl.BlockSpec((1, tb, MISC_W), blk),
                  pl.BlockSpec((MISC_W, 2 * HK), const),
                  pl.BlockSpec((1, HK), const),
                  pl.BlockSpec((1, dv), const)],
        out_specs=pl.BlockSpec((1, tb, HV), blk),
        scratch_shapes=[pltpu.VMEM((GLA_HEADS, dv, dk), F32)],
        compiler_params=_params(2),
        name="gla",
    )(qb, kb, vb, gb, misc, wg_pad, b_gate.reshape(1, -1), gla_norm_g.reshape(1, -1))


def _pack_bf16_pairs(x):
    half = x.shape[1] // 2
    bits = lax.bitcast_convert_type(x.astype(BF16).astype(F32), I32)
    return (bits[:, half:] & jnp.int32(-65536)) | lax.shift_right_logical(bits[:, :half], 16)


def _unpack_bf16_pairs(w):
    lo = lax.bitcast_convert_type(w << 16, F32)
    hi = lax.bitcast_convert_type(w & jnp.int32(-65536), F32)
    return jnp.concatenate([lo, hi], axis=1)


def _merge_kernel(ya_ref, yb_ref, ga_ref, gbm_ref, x_ref, gt_ref, sc_ref, sh_ref, g2_ref,
                  wa_ref, wb_ref, wo_ref, rw_ref, rb_ref,
                  x1_ref, h2_ref, e_ref, w_ref, cnt_ref, *, n_exp):
    pa = jnp.dot(ya_ref[0], wa_ref[...], preferred_element_type=F32)
    pb = jnp.dot(yb_ref[0], wb_ref[...], preferred_element_type=F32)
    merged = (jax.nn.sigmoid(ga_ref[0].astype(F32)) * pa
              + jax.nn.sigmoid(gbm_ref[0].astype(F32)) * pb)
    mo = jnp.dot(merged.astype(BF16), wo_ref[...], preferred_element_type=F32)
    x1 = x_ref[0] + gt_ref[0] * mo
    x1_ref[0] = x1
    h2 = _rms_mod(x1, g2_ref[...], sc_ref[0], sh_ref[0])
    h2_ref[0] = _pack_bf16_pairs(h2)

    h_hi = h2.astype(BF16)
    h_lo = (h2 - h_hi.astype(F32)).astype(BF16)
    part = jnp.dot(h_hi, rw_ref[...], preferred_element_type=F32)
    logits = (part[:, :LANES] + part[:, LANES:]
              + jnp.dot(h_lo, rw_ref[:, :LANES], preferred_element_type=F32)) + rb_ref[...]
    tm = logits.shape[0]
    lane = lax.broadcasted_iota(I32, (tm, LANES), 1)
    logits = jnp.where(lane < n_exp, logits, -jnp.inf)
    top_v, top_e = [], []
    for _ in range(TOP_K):
        mx = jnp.max(logits, axis=1, keepdims=True)
        idx = jnp.min(jnp.where(logits == mx, lane, LANES), axis=1, keepdims=True)
        logits = jnp.where(lane == idx, -jnp.inf, logits)
        top_v.append(mx)
        top_e.append(idx)
    ex = [jnp.exp(v - top_v[0]) for v in top_v]
    den = ex[0]
    for t in ex[1:]:
        den = den + t
    e_out = jnp.zeros((tm, LANES), I32)
    w_out = jnp.zeros((tm, LANES), F32)
    hot = jnp.zeros((tm, LANES), F32)
    for k in range(TOP_K):
        e_out = jnp.where(lane == k, top_e[k], e_out)
        w_out = jnp.where(lane == k, ex[k] / den, w_out)
        hot = hot + jnp.where(lane == top_e[k], 1.0, 0.0)
    e_ref[0] = e_out
    w_ref[0] = w_out

    @pl.when(jnp.logical_and(pl.program_id(0) == 0, pl.program_id(1) == 0))
    def _():
        cnt_ref[...] = jnp.zeros(cnt_ref.shape, F32)

    cnt_ref[0:1, :] += jnp.sum(hot, axis=0, keepdims=True)


def _merge(ya, yb, ga, gbm, x, gt1, sc2, sh2, g2, wa, wb, wo, rw_pad, rb_pad, n_exp, tm):
    B, T, D = x.shape
    row = lambda b, i: (b, i, 0)
    per_b = lambda b, i: (b, 0, 0)
    const = lambda b, i: (0, 0)
    resident = functools.partial(pl.BlockSpec, index_map=const, pipeline_mode=pl.Buffered(1))
    return pl.pallas_call(
        functools.partial(_merge_kernel, n_exp=n_exp),
        out_shape=[jax.ShapeDtypeStruct((B, T, D), F32),
                   jax.ShapeDtypeStruct((B, T, D // 2), I32),
                   jax.ShapeDtypeStruct((B, T, LANES), I32),
                   jax.ShapeDtypeStruct((B, T, LANES), F32),
                   jax.ShapeDtypeStruct((8, LANES), F32)],
        grid=(B, T // tm),
        in_specs=[pl.BlockSpec((1, tm, D), row),
                  pl.BlockSpec((1, tm, D), row),
                  pl.BlockSpec((1, tm, D), row),
                  pl.BlockSpec((1, tm, D), row),
                  pl.BlockSpec((1, tm, D), row),
                  pl.BlockSpec((1, 1, D), per_b),
                  pl.BlockSpec((1, 1, D), per_b),
                  pl.BlockSpec((1, 1, D), per_b),
                  pl.BlockSpec((1, D), const),
                  resident((D, D)), resident((D, D)), resident((D, D)),
                  resident((D, 2 * LANES)), pl.BlockSpec((1, LANES), const)],
        out_specs=[pl.BlockSpec((1, tm, D), row),
                   pl.BlockSpec((1, tm, D // 2), row),
                   pl.BlockSpec((1, tm, LANES), row),
                   pl.BlockSpec((1, tm, LANES), row),
                   pl.BlockSpec((8, LANES), const)],
        compiler_params=_params(2),
        name="merge",
    )(ya, yb, ga, gbm, x, gt1, sc2, sh2, g2, wa, wb, wo, rw_pad, rb_pad)


def _plan_kernel(e_ref, cnt_ref, slot_ref, meta_ref, base_ref, *, n_exp, tmb):
    tt = e_ref.shape[0]
    lane1 = lax.broadcasted_iota(I32, (1, LANES), 1)

    @pl.when(pl.program_id(0) == 0)
    def _():
        cnt = cnt_ref[0:1, :]
        padded = jnp.where(lane1 < n_exp, jnp.ceil(cnt / tmb) * tmb, 0.0)
        r = lax.broadcasted_iota(I32, (LANES, LANES), 0)
        c = lax.broadcasted_iota(I32, (LANES, LANES), 1)
        upper = jnp.where(r < c, 1.0, 0.0)
        start = jnp.dot(jnp.broadcast_to(padded, (8, LANES)), upper, precision=HI,
                        preferred_element_type=F32)[0:1, :]
        base_ref[...] = start
        total = jnp.broadcast_to(jnp.sum(padded, axis=1, keepdims=True), (1, LANES))
        row = lax.broadcasted_iota(I32, (8, LANES), 0)
        meta = jnp.where(row == 0, start, jnp.where(row == 1, padded, jnp.where(row == 2, total, 0.0)))
        meta_ref[...] = (meta / tmb).astype(I32)

    e = e_ref[...]
    lane = lax.broadcasted_iota(I32, (tt, LANES), 1)
    hot = jnp.zeros((tt, LANES), F32)
    for k in range(TOP_K):
        hot = hot + jnp.where(lane == e[:, k:k + 1], 1.0, 0.0)
    r = lax.broadcasted_iota(I32, (tt, tt), 0)
    c = lax.broadcasted_iota(I32, (tt, tt), 1)
    lower = jnp.where(r > c, 1.0, 0.0).astype(BF16)
    rank = jnp.dot(lower, hot.astype(BF16), preferred_element_type=F32)
    pos = base_ref[...] + rank
    out = jnp.zeros((tt, LANES), F32)
    for k in range(TOP_K):
        sk = jnp.sum(jnp.where(lane == e[:, k:k + 1], pos, 0.0), axis=1, keepdims=True)
        out = jnp.where(lane == k, sk, out)
    slot_ref[...] = out.astype(I32)
    base_ref[...] += jnp.sum(hot, axis=0, keepdims=True)


def _plan(e_pad, cnt, n_exp, tmb, tt):
    N = e_pad.shape[0]
    return pl.pallas_call(
        functools.partial(_plan_kernel, n_exp=n_exp, tmb=tmb),
        out_shape=[jax.ShapeDtypeStruct((N, LANES), I32),
                   jax.ShapeDtypeStruct((8, LANES), I32)],
        grid=(N // tt,),
        in_specs=[pl.BlockSpec((tt, LANES), lambda i: (i, 0)),
                  pl.BlockSpec((8, LANES), lambda i: (0, 0))],
        out_specs=[pl.BlockSpec((tt, LANES), lambda i: (i, 0)),
                   pl.BlockSpec((8, LANES), lambda i: (0, 0))],
        scratch_shapes=[pltpu.VMEM((1, LANES), F32)],
        compiler_params=_params(1),
        name="moe_plan",
    )(e_pad, cnt)


def _dispatch_kernel(slot_ref, h_ref, xs_in_ref, xs_ref, sem, *, tt):
    del xs_in_ref

    def start(g, carry):
        t0 = pl.multiple_of(g * ROW_GROUP, ROW_GROUP)
        for j in range(ROW_GROUP):
            for k in range(TOP_K):
                s = slot_ref[(t0 + j) * TOP_K + k]
                pltpu.make_async_copy(h_ref.at[pl.ds(t0 + j, 1)], xs_ref.at[pl.ds(s, 1)],
                                      sem).start(priority=k % 2)
        return carry

    lax.fori_loop(0, tt // ROW_GROUP, start, 0)
    for k in range(TOP_K):
        pltpu.make_async_copy(h_ref, xs_ref.at[pl.ds(0, tt)], sem).wait()


def _dispatch(slots_flat, h2, xs0, tt):
    N, D = h2.shape
    return pl.pallas_call(
        functools.partial(_dispatch_kernel, tt=tt),
        out_shape=jax.ShapeDtypeStruct(xs0.shape, xs0.dtype),
        grid=(N // tt,),
        in_specs=[pl.BlockSpec((tt * TOP_K,), lambda i: (i,), memory_space=pltpu.SMEM),
                  pl.BlockSpec((tt, D), lambda i: (i, 0)),
                  pl.BlockSpec(memory_space=pl.ANY)],
        out_specs=pl.BlockSpec(memory_space=pl.ANY),
        scratch_shapes=[pltpu.SemaphoreType.DMA],
        input_output_aliases={2: 0},
        compiler_params=_params(1),
        name="moe_dispatch",
    )(slots_flat, h2, xs0)


def _experts_kernel(first_ref, nblk_ref, total_ref, xs_ref, wgu_ref, bgu_ref, wd_ref, bd_ref,
                    ys_ref, xbuf, ybuf, wgu_bf, wd_bf, sem_in, sem_out, *, ff, tmb):
    e = pl.program_id(0)
    b0 = first_ref[e]
    nb = nblk_ref[e]

    def rows(b):
        return pl.ds(pl.multiple_of((b0 + b) * tmb, tmb), tmb)

    def in_copy(b):
        return pltpu.make_async_copy(xs_ref.at[rows(b)], xbuf.at[b & 1], sem_in.at[b & 1])

    def out_copy(b):
        return pltpu.make_async_copy(ybuf.at[b & 1], ys_ref.at[rows(b)], sem_out.at[b & 1])

    @pl.when(nb > 0)
    def _():
        in_copy(0).start()
        wgu_bf[...] = wgu_ref[0].astype(BF16)
        wd_bf[...] = wd_ref[0].astype(BF16)

        def block(b, carry):
            @pl.when(b + 1 < nb)
            def _():
                in_copy(b + 1).start()

            in_copy(b).wait()

            @pl.when(b >= 2)
            def _():
                out_copy(b - 2).wait()

            x = _unpack_bf16_pairs(xbuf[b & 1]).astype(BF16)
            hgu = jnp.dot(x, wgu_bf[...], preferred_element_type=F32) + bgu_ref[0]
            gate = jnp.minimum(hgu[:, :ff], SWIGLU_LIMIT)
            up = jnp.clip(hgu[:, ff:], -SWIGLU_LIMIT, SWIGLU_LIMIT)
            act = (up + 1.0) * (gate * jax.nn.sigmoid(SWIGLU_ALPHA * gate))
            y = jnp.dot(act.astype(BF16), wd_bf[...], preferred_element_type=F32) + bd_ref[0]
            ybuf[b & 1] = _pack_bf16_pairs(y)
            out_copy(b).start()
            return carry

        lax.fori_loop(0, nb, block, 0)

        @pl.when(nb >= 2)
        def _():
            out_copy(nb - 2).wait()

        out_copy(nb - 1).wait()

    @pl.when(e == pl.num_programs(0) - 1)
    def _():
        ybuf[0] = jnp.zeros(ybuf.shape[1:], I32)

        def tail(j, carry):
            cp = pltpu.make_async_copy(ybuf.at[0], ys_ref.at[pl.ds(pl.multiple_of(j * tmb, tmb), tmb)],
                                       sem_out.at[0])
            cp.start()
            cp.wait()
            return carry

        lax.fori_loop(total_ref[0], ys_ref.shape[0] // tmb, tail, 0)


def _experts(first_blk, nblk, total, xs, w_gate_up, b_gate_up, w_down, b_down, tmb):
    P, dp = xs.shape
    E, D, ff2 = w_gate_up.shape
    ff = ff2 // 2
    by_expert = lambda e, *_: (e, 0, 0)
    return pl.pallas_call(
        functools.partial(_experts_kernel, ff=ff, tmb=tmb),
        out_shape=jax.ShapeDtypeStruct((P, dp), I32),
        grid_spec=pltpu.PrefetchScalarGridSpec(
            num_scalar_prefetch=3,
            grid=(E,),
            in_specs=[pl.BlockSpec(memory_space=pl.ANY),
                      pl.BlockSpec((1, D, ff2), by_expert),
                      pl.BlockSpec((1, 1, ff2), by_expert),
                      pl.BlockSpec((1, ff, D), by_expert),
                      pl.BlockSpec((1, 1, D), by_expert)],
            out_specs=pl.BlockSpec(memory_space=pl.ANY),
            scratch_shapes=[pltpu.VMEM((2, tmb, dp), I32), pltpu.VMEM((2, tmb, dp), I32),
                            pltpu.VMEM((D, ff2), BF16), pltpu.VMEM((ff, D), BF16),
                            pltpu.SemaphoreType.DMA((2,)), pltpu.SemaphoreType.DMA((2,))]),
        compiler_params=_params(1),
        name="moe_experts",
    )(first_blk, nblk, total, xs, w_gate_up, b_gate_up.reshape(E, 1, ff2), w_down,
      b_down.reshape(E, 1, D))


def _combine_kernel(slot_ref, ys_ref, w_ref, x1_ref, gt_ref, fg_ref, o_ref, buf, sem, *, tt, nt):
    step = pl.program_id(0) * nt + pl.program_id(1)
    nsteps = pl.num_programs(0) * nt

    def issue(s):
        b = s & 1

        def body(g, carry):
            t0 = pl.multiple_of(g * ROW_GROUP, ROW_GROUP)
            for j in range(ROW_GROUP):
                for k in range(TOP_K):
                    idx = slot_ref[(s * tt + t0 + j) * TOP_K + k]
                    pltpu.make_async_copy(ys_ref.at[pl.ds(idx, 1)], buf.at[b, k, pl.ds(t0 + j, 1)],
                                          sem.at[b]).start(priority=k % 2)
            return carry
        lax.fori_loop(0, tt // ROW_GROUP, body, 0)

    @pl.when(step == 0)
    def _():
        issue(step)

    @pl.when(step + 1 < nsteps)
    def _():
        issue(step + 1)

    cur = step & 1
    for k in range(TOP_K):
        pltpu.make_async_copy(ys_ref.at[pl.ds(0, tt)], buf.at[cur, k], sem.at[cur]).wait()

    w = w_ref[0]
    moe = _unpack_bf16_pairs(buf[cur, 0]) * w[:, 0:1]
    for k in range(1, TOP_K):
        moe = moe + _unpack_bf16_pairs(buf[cur, k]) * w[:, k:k + 1]
    x2 = x1_ref[0] + gt_ref[0] * moe
    y = x2 * lax.rsqrt(jnp.mean(x2 * x2, axis=-1, keepdims=True) + EPS)
    o_ref[0] = y * fg_ref[...]


def _combine(slots_flat, ys, w_pad, x1, gt2, final_g, tt):
    B, T, D = x1.shape
    nt = T // tt
    row = lambda b, i, slots: (b, i, 0)
    return pl.pallas_call(
        functools.partial(_combine_kernel, tt=tt, nt=nt),
        out_shape=jax.ShapeDtypeStruct((B, T, D), F32),
        grid_spec=pltpu.PrefetchScalarGridSpec(
            num_scalar_prefetch=1,
            grid=(B, nt),
            in_specs=[pl.BlockSpec(memory_space=pl.ANY),
                      pl.BlockSpec((1, tt, LANES), row),
                      pl.BlockSpec((1, tt, D), row),
                      pl.BlockSpec((1, 1, D), lambda b, i, slots: (b, 0, 0)),
                      pl.BlockSpec((1, D), lambda b, i, slots: (0, 0))],
            out_specs=pl.BlockSpec((1, tt, D), row),
            scratch_shapes=[pltpu.VMEM((2, TOP_K, tt, ys.shape[1]), ys.dtype),
                            pltpu.SemaphoreType.DMA((2,))]),
        compiler_params=_params(2),
        name="moe_combine",
    )(slots_flat, ys, w_pad, x1, gt2, final_g.reshape(1, -1))


def _layer(x, c, w_ada, b_ada, norm1_g, w_in, w_gate_lr, b_gate, gla_norm_g, w_branch_a,
           w_branch_b, w_out, norm2_g, router_w, router_b, w_gate_up, b_gate_up, w_down, b_down,
           final_g):
    B, T, D = x.shape
    N = B * T
    n_exp = router_w.shape[1]
    hk = w_gate_lr.shape[1]
    hv = w_branch_b.shape[0]
    ni = IDX_HEADS * IDX_DIM

    c8 = jnp.zeros((8, D), F32).at[:B].set(c)
    mod = _adaln(c8, w_ada, b_ada)[:B]
    sh1, sc1, gt1, sh2, sc2, gt2 = [m.reshape(B, 1, D) for m in jnp.split(mod, 6, axis=-1)]

    sizes = (D, D, D, ni, IDX_DIM, IDX_HEADS, hk, hk, hv, hv, GATE_RANK, D, D)
    offs = [0]
    for s in sizes:
        offs.append(offs[-1] + s)
    seg = lambda j: w_in[:, offs[j]:offs[j + 1]]
    groups = [seg(1), seg(6), seg(7), seg(8), seg(9), seg(11), seg(12)]
    widths = tuple(g.shape[1] for g in groups)
    w_main = jnp.concatenate(groups, axis=1).astype(BF16)
    t_groups = [seg(0), seg(3), seg(2)]
    t_heights = tuple(g.shape[1] for g in t_groups)
    t_out_heights = (t_heights[0], IDX_HEADS * LANES, t_heights[2])
    w_t = jnp.concatenate(t_groups, axis=1).T.astype(BF16)
    w_misc = jnp.concatenate(
        [seg(4), seg(5), seg(10), jnp.zeros((D, MISC_W - IDX_DIM - IDX_HEADS - GATE_RANK), F32)],
        axis=1)
    ka, qb, kb, vb, gb, ga, gbm, qa_t, qi_t, va_t, misc, misc_t = _inproj(
        x, norm1_g.reshape(1, D), sc1, sh1, w_main, w_t, _split_bf16(w_misc), widths, t_heights,
        t_out_heights, min(256, T))

    ya = _dsa(qa_t, ka, va_t, qi_t, misc, misc_t, min(512, T))

    wg_pad = jnp.zeros((MISC_W, hk), F32).at[LR_OFF:LR_OFF + GATE_RANK].set(w_gate_lr)
    yb = _gla(qb, kb, vb, gb, misc, _split_bf16(wg_pad), b_gate, gla_norm_g, min(256, T))

    rw_pad = _split_bf16(jnp.zeros((D, LANES), F32).at[:, :n_exp].set(router_w))
    rb_pad = jnp.zeros((1, LANES), F32).at[0, :n_exp].set(router_b)
    x1, h2, e_pad, w_pad, cnt = _merge(
        ya, yb, ga, gbm, x, gt1, sc2, sh2, norm2_g.reshape(1, D),
        w_branch_a.astype(BF16), w_branch_b.astype(BF16), w_out.astype(BF16),
        rw_pad, rb_pad, n_exp, min(512, T))

    tmb = 256
    nb = -(-N * TOP_K // tmb) + n_exp
    slots, meta = _plan(e_pad.reshape(N, LANES), cnt, n_exp, tmb, min(512, N))

    tt_d = min(256, T)
    slots_flat = slots[:, :TOP_K].reshape(-1)
    xs = _dispatch(slots_flat, h2.reshape(N, D // 2), jnp.zeros((nb * tmb, D // 2), I32), tt_d)
    ys = _experts(meta[0, :n_exp], meta[1, :n_exp], meta[2, :1], xs, w_gate_up, b_gate_up, w_down,
                  b_down, tmb)
    tt_c = min(256, T)
    return _combine(slots_flat, ys, w_pad, x1, gt2, final_g, tt_c)


def kernel(x, c, w_ada, b_ada, norm1_g, w_in, w_gate_lr, b_gate, gla_norm_g, w_branch_a, w_branch_b,
           w_out, norm2_g, router_w, router_b, w_gate_up, b_gate_up, w_down, b_down, final_g):
    assert w_ada.shape[0] == 1, "single-layer block"
    return _layer(x, c, w_ada[0], b_ada[0], norm1_g[0], w_in[0], w_gate_lr[0], b_gate[0],
                  gla_norm_g[0], w_branch_a[0], w_branch_b[0], w_out[0], norm2_g[0], router_w[0],
                  router_b[0], w_gate_up[0], b_gate_up[0], w_down[0], b_down[0], final_g)
```

```python
import functools

import jax
import jax.numpy as jnp
from jax import lax
from jax.experimental import pallas as pl
from jax.experimental.pallas import tpu as pltpu

CHUNK = 64
Q_BLOCK = 128
DSA_HEADS = 8
IDX_HEADS = 8
IDX_DIM = 64
IDX_TOPK_MAX = 256
GLA_HEADS = 4
GATE_RANK = 16
GATE_TAU = 16.0
TOP_K = 4
SWIGLU_LIMIT = 7.0
SWIGLU_ALPHA = 1.702
EPS = 1e-6

LANES = 128
ATT_UNROLL = 4
ROW_GROUP = 8
MISC_W = LANES
KI_OFF, WI_OFF, LR_OFF = 0, IDX_DIM, IDX_DIM + IDX_HEADS
NEG_SCORE = -3.0e38
NEG_BIAS = -1.0e30
VMEM_LIMIT = 56 * 1024 * 1024

F32 = jnp.float32
BF16 = jnp.bfloat16
I32 = jnp.int32
HI = lax.Precision.HIGHEST


def _params(n_axes, vmem=VMEM_LIMIT):
    return pltpu.CompilerParams(dimension_semantics=("arbitrary",) * n_axes,
                                vmem_limit_bytes=vmem)


def _log2(n):
    b = n.bit_length() - 1
    assert (1 << b) == n, n
    return b


def _adaln_kernel(c_ref, w_ref, b_ref, o_ref):
    c = c_ref[...]
    s = c * jax.nn.sigmoid(c)
    o_ref[...] = jnp.dot(s, w_ref[...], precision=HI, preferred_element_type=F32) + b_ref[...]


def _adaln(c8, w_ada, b_ada):
    D = c8.shape[1]
    n = w_ada.shape[1] // D
    return pl.pallas_call(
        _adaln_kernel,
        out_shape=jax.ShapeDtypeStruct((c8.shape[0], n * D), F32),
        grid=(n,),
        in_specs=[pl.BlockSpec((c8.shape[0], D), lambda j: (0, 0)),
                  pl.BlockSpec((D, D), lambda j: (0, j)),
                  pl.BlockSpec((1, D), lambda j: (0, j))],
        out_specs=pl.BlockSpec((c8.shape[0], D), lambda j: (0, j)),
        compiler_params=_params(1),
        name="adaln",
    )(c8, w_ada, b_ada.reshape(1, -1))


def _rms_mod(x, g, sc, sh):
    y = x * lax.rsqrt(jnp.mean(x * x, axis=-1, keepdims=True) + EPS)
    return (y * g) * (1.0 + sc) + sh


_CONTRACT_LAST = (((1,), (1,)), ((), ()))
_CONTRACT_FIRST = (((0,), (0,)), ((), ()))


def _inproj_kernel(x_ref, g_ref, sc_ref, sh_ref, w_ref, wt_ref, wm_ref, wmt_ref, *out_refs,
                   widths, t_heights, t_scales):
    n_tok = len(widths)
    h = _rms_mod(x_ref[0], g_ref[...], sc_ref[0], sh_ref[0])
    hb = h.astype(BF16)
    ht = h.T
    htb = ht.astype(BF16)
    h_lo = (h - hb.astype(F32)).astype(BF16)
    ht_lo = (ht - htb.astype(F32)).astype(BF16)
    pt = jnp.dot(wmt_ref[...], htb, preferred_element_type=F32)
    out_refs[-1][0] = (pt[:MISC_W] + pt[MISC_W:]
                       + jnp.dot(wmt_ref[:MISC_W, :], ht_lo, preferred_element_type=F32))
    pm = jnp.dot(hb, wm_ref[...], preferred_element_type=F32)
    out_refs[-2][0] = (pm[:, :MISC_W] + pm[:, MISC_W:]
                       + jnp.dot(h_lo, wm_ref[:, :MISC_W], preferred_element_type=F32))
    off = 0
    for o_ref, w in zip(out_refs[:n_tok], widths):
        o_ref[0] = jnp.dot(hb, w_ref[:, off:off + w], preferred_element_type=F32).astype(BF16)
        off += w
    off = 0
    for o_ref, hgt, scale in zip(out_refs[n_tok:-2], t_heights, t_scales):
        val = jnp.dot(wt_ref[off:off + hgt, :], htb, preferred_element_type=F32)
        val = (val if scale == 1.0 else val * scale).astype(BF16)
        if len(o_ref.shape) == 4:
            o_ref[0, 0] = val
        elif o_ref.shape[1] == hgt:
            o_ref[0] = val
        else:
            for j in range(hgt // IDX_DIM):
                o_ref[0, j * LANES:j * LANES + IDX_DIM, :] = val[j * IDX_DIM:(j + 1) * IDX_DIM, :]
                o_ref[0, j * LANES + IDX_DIM:(j + 1) * LANES, :] = jnp.zeros(
                    (LANES - IDX_DIM, val.shape[1]), BF16)
        off += hgt


def _inproj(x, g, sc, sh, w_main, w_t, w_misc, widths, t_heights, t_out_heights, t_scales, tm):
    B, T, D = x.shape
    row = lambda b, i: (b, i, 0)
    col = lambda b, i: (b, 0, i)
    per_b = lambda b, i: (b, 0, 0)
    const = lambda b, i: (0, 0)
    resident = functools.partial(pl.BlockSpec, index_map=const, pipeline_mode=pl.Buffered(1))
    out_shape = [jax.ShapeDtypeStruct((B, T, w), BF16) for w in widths]
    out_specs = [pl.BlockSpec((1, tm, w), row) for w in widths]
    out_shape += [jax.ShapeDtypeStruct((B, hgt, T), BF16) for hgt in t_out_heights[:-1]]
    out_specs += [pl.BlockSpec((1, hgt, tm), col) for hgt in t_out_heights[:-1]]
    out_shape += [jax.ShapeDtypeStruct((B, T // tm, t_heights[-1], tm), BF16),
                  jax.ShapeDtypeStruct((B, T, MISC_W), F32),
                  jax.ShapeDtypeStruct((B, MISC_W, T), F32)]
    out_specs += [pl.BlockSpec((1, 1, t_heights[-1], tm), lambda b, i: (b, i, 0, 0)),
                  pl.BlockSpec((1, tm, MISC_W), row),
                  pl.BlockSpec((1, MISC_W, tm), col)]
    return pl.pallas_call(
        functools.partial(_inproj_kernel, widths=widths, t_heights=t_heights, t_scales=t_scales),
        out_shape=out_shape,
        grid=(B, T // tm),
        in_specs=[pl.BlockSpec((1, tm, D), row),
                  pl.BlockSpec((1, D), const),
                  pl.BlockSpec((1, 1, D), per_b),
                  pl.BlockSpec((1, 1, D), per_b),
                  resident(w_main.shape), resident(w_t.shape),
                  resident((D, 2 * MISC_W)), resident((2 * MISC_W, D))],
        out_specs=out_specs,
        compiler_params=_params(2),
        name="inproj",
    )(x, g, sc, sh, w_main, w_t, w_misc, w_misc.T)


def _split_bf16(w):
    hi = w.astype(BF16)
    return jnp.concatenate([hi, (w - hi.astype(F32)).astype(BF16)], axis=-1)


def _dsa_kernel(q_ref, k_ref, vt_ref, qi_ref, mall_ref, mt_ref, o_ref, sc_ref, lg0_ref, lg1_ref,
                acc_ref, *, T, DQ, KT, KC, topk, dh):
    i = pl.program_id(1)
    lg_refs = (lg0_ref, lg1_ref)
    nt = ((i + 1) * DQ + KT - 1) // KT
    qpos = i * DQ + lax.broadcasted_iota(I32, (1, DQ), 1)
    lim = ((qpos >> _log2(CHUNK)) + 1) << _log2(CHUNK)
    krow = lax.broadcasted_iota(I32, (KT, DQ), 0)

    def tile(kt):
        return pl.ds(pl.multiple_of(kt * KT, KT), KT)

    def half_tile(kt):
        return pl.ds(pl.multiple_of(kt * (KT // 2), KT // 2), KT // 2)

    def trunc_bf16(x):
        bits = lax.bitcast_convert_type(x, I32) & jnp.int32(-65536)
        return lax.bitcast_convert_type(bits, F32).astype(BF16)

    wi_t = mt_ref[0, WI_OFF:WI_OFF + IDX_HEADS, :] * ((IDX_HEADS ** -0.5) * (IDX_DIM ** -0.5))

    def score_tile(kt, carry):
        ki_t = mall_ref[0, tile(kt), :].astype(BF16)
        s = jnp.zeros((KT, DQ), F32)
        for h in range(IDX_HEADS):
            x = jnp.dot(ki_t, qi_ref[0, h * LANES:(h + 1) * LANES, :], preferred_element_type=F32)
            s = s + jnp.maximum(x, 0.0) * wi_t[h:h + 1, :]
        s = jnp.where(krow + kt * KT < lim, s, NEG_SCORE)
        sc_ref[tile(kt), :] = s
        lg1_ref[half_tile(kt), :] = pltpu.bitcast(trunc_bf16(s), F32)
        return carry

    lax.fori_loop(0, nt, score_tile, 0)

    def count(pred):
        def body(kt, acc):
            m = jnp.where(pred(sc_ref[tile(kt), :], kt * KT), 1.0, 0.0)
            return acc + jnp.sum(m, axis=0, keepdims=True)
        return lax.fori_loop(0, nt, body, jnp.zeros((1, DQ), F32))

    int_min = jnp.int32(-2 ** 31)

    def key_to_float(u):
        bits = jnp.where(u < 0, u ^ int_min, ~u)
        return lax.bitcast_convert_type(bits, F32)

    one_b, zero_b = jnp.ones((), BF16), jnp.zeros((), BF16)
    rows_b = 16

    def count_hi(tb):
        def body(kt, acc):
            hi = pltpu.bitcast(lg1_ref[half_tile(kt), :], BF16)
            m = jnp.where(hi >= tb, one_b, zero_b)
            for j in range(KT // rows_b):
                acc = acc + m[j * rows_b:(j + 1) * rows_b, :]
            return acc
        acc = lax.fori_loop(0, nt, body, jnp.zeros((rows_b, DQ), BF16))
        return jnp.sum(acc.astype(F32), axis=0, keepdims=True)

    def hi_body(it, carry):
        u, c_ge = carry
        trial = u | (jnp.int32(1) << (31 - it))
        cnt = count_hi(trunc_bf16(key_to_float(trial)))
        ok = cnt >= topk
        return jnp.where(ok, trial, u), jnp.where(ok, cnt, c_ge)

    def lo_body(it, carry):
        u, c_ge = carry
        trial = u | (jnp.int32(1) << (15 - it))
        t = key_to_float(trial)
        cnt = count(lambda s, c0: s >= t)
        ok = cnt >= topk
        return jnp.where(ok, trial, u), jnp.where(ok, cnt, c_ge)

    carry = (jnp.zeros((1, DQ), I32), jnp.zeros((1, DQ), F32))
    carry = lax.fori_loop(0, 16, hi_body, carry)
    u, c_ge = lax.fori_loop(0, 16, lo_body, carry)
    thr = key_to_float(u)

    c_gt = count(lambda s, c0: s > thr)
    need = topk - c_gt
    excess = jnp.logical_and(c_ge - c_gt > need, thr > NEG_SCORE)
    has_excess = jnp.max(jnp.where(excess, 1.0, 0.0)) > 0.0

    def store_bias(kt, sel):
        sel = jnp.logical_and(sel, krow + kt * KT < lim)
        sc_ref[tile(kt), :] = jnp.where(sel, 0.0, NEG_BIAS)

    def bias_no_ties():
        def body(kt, carry):
            store_bias(kt, sc_ref[tile(kt), :] >= thr)
            return carry
        lax.fori_loop(0, nt, body, 0)

    def bias_with_ties():
        r = lax.broadcasted_iota(I32, (KT, KT), 0)
        c = lax.broadcasted_iota(I32, (KT, KT), 1)
        before = jnp.where(c < r, 1.0, 0.0).astype(BF16)

        def body(kt, seen):
            s = sc_ref[tile(kt), :]
            eq = jnp.where(s == thr, 1.0, 0.0)
            rank = seen + jnp.dot(before, eq.astype(BF16), preferred_element_type=F32)
            store_bias(kt, jnp.logical_or(s > thr, jnp.logical_and(s == thr, rank < need)))
            return seen + jnp.sum(eq, axis=0, keepdims=True)
        lax.fori_loop(0, nt, body, jnp.zeros((1, DQ), F32))

    lax.cond(has_excess, bias_with_ties, bias_no_ties)

    def fused(h_b, h_a, kts, mx_b, den, mx_a):
        if h_a is not None:
            hs_a = slice(h_a * dh, (h_a + 1) * dh)
            lgs = [jnp.dot(k_ref[0, tile(kt), hs_a], q_ref[0, hs_a, :], preferred_element_type=F32)
                   for kt in kts]
        if h_b is not None:
            hs_b = slice(h_b * dh, (h_b + 1) * dh)
            ps = [jnp.exp2(lg_refs[h_b % 2][tile(kt), :] - mx_b) for kt in kts]
            pv = jnp.zeros((dh, DQ), F32)
            for kt, p in zip(kts, ps):
                pb = p.astype(BF16)
                for j in range(KT // KC):
                    pv = pv + jnp.dot(vt_ref[0, kt * (KT // KC) + j, hs_b, :],
                                      pb[j * KC:(j + 1) * KC, :], preferred_element_type=F32)
                den = den + jnp.sum(p, axis=0, keepdims=True)
            acc_ref[hs_b, :] += pv
        if h_a is not None:
            for kt, lg in zip(kts, lgs):
                s = lg + sc_ref[tile(kt), :]
                lg_refs[h_a % 2][tile(kt), :] = s
                mx_a = jnp.maximum(mx_a, jnp.max(s, axis=0, keepdims=True))
        return den, mx_a

    def tile_loop(h_b, h_a, mx_b):
        def run(first, count, carry):
            for i in range(count):
                carry = fused(h_b, h_a, [first + i], mx_b, *carry)
            return carry

        carry = lax.fori_loop(0, nt // ATT_UNROLL,
                              lambda j, c: run(ATT_UNROLL * j, ATT_UNROLL, c), (den0, mx0))
        done = (nt // ATT_UNROLL) * ATT_UNROLL
        width = ATT_UNROLL // 2
        while width >= 1:
            take = (nt & width) != 0
            carry = lax.cond(take, functools.partial(run, done, width), lambda c: c, carry)
            done = done + jnp.where(take, width, 0)
            width //= 2
        return carry

    mx0 = jnp.full((1, DQ), NEG_BIAS, F32)
    den0 = jnp.zeros((1, DQ), F32)
    acc_ref[...] = jnp.zeros(acc_ref.shape, F32)
    _, mx = tile_loop(None, 0, None)
    for h in range(DSA_HEADS):
        den, mx_next = tile_loop(h, h + 1 if h + 1 < DSA_HEADS else None, mx)
        hs = slice(h * dh, (h + 1) * dh)
        o_ref[0, :, hs] = (acc_ref[hs, :] / den).T.astype(BF16)
        mx = mx_next


def _dsa(qa_t, ka, va_t, qi_t, misc, misc_t, dq):
    B, T, D = ka.shape
    dh = D // DSA_HEADS
    KC = va_t.shape[3]
    KT = min(512, T)
    topk = min(IDX_TOPK_MAX, T // 4)
    assert KT >= topk and KT % KC == 0 and dq % CHUNK == 0
    assert T // 16 <= 256, "packed bf16 counters must stay exact"
    blk = lambda b, i: (b, i, 0)
    whole = functools.partial(pl.BlockSpec, pipeline_mode=pl.Buffered(1))
    return pl.pallas_call(
        functools.partial(_dsa_kernel, T=T, DQ=dq, KT=KT, KC=KC, topk=topk, dh=dh),
        out_shape=jax.ShapeDtypeStruct((B, T, D), BF16),
        grid=(B, T // dq),
        in_specs=[pl.BlockSpec((1, D, dq), lambda b, i: (b, 0, i)),
                  whole((1, T, D), lambda b, i: (b, 0, 0)),
                  whole((1, T // KC, D, KC), lambda b, i: (b, 0, 0, 0)),
                  pl.BlockSpec((1, IDX_HEADS * LANES, dq), lambda b, i: (b, 0, i)),
                  whole((1, T, MISC_W), lambda b, i: (b, 0, 0)),
                  pl.BlockSpec((1, MISC_W, dq), lambda b, i: (b, 0, i))],
        out_specs=pl.BlockSpec((1, dq, D), blk),
        scratch_shapes=[pltpu.VMEM((T, dq), F32),
                        pltpu.VMEM((T, dq), F32),
                        pltpu.VMEM((T, dq), F32),
                        pltpu.VMEM((D, dq), F32)],
        compiler_params=_params(2),
        name="dsa",
    )(qa_t, ka, va_t, qi_t, misc, misc_t)


def _gla_kernel(q_ref, k_ref, v_ref, gb_ref, misc_ref, wg_ref, bg_ref, gn_ref, o_ref, st_ref,
                *, tb, dk, dv):
    @pl.when(pl.program_id(1) == 0)
    def _():
        st_ref[...] = jnp.zeros(st_ref.shape, F32)

    hk = wg_ref.shape[1] // 2
    nbb = q_ref.shape[0]
    las = []
    for bb in range(nbb):
        m = misc_ref[bb]
        m_hi = m.astype(BF16)
        m_lo = (m - m_hi.astype(F32)).astype(BF16)
        px = jnp.dot(m_hi, wg_ref[...], preferred_element_type=F32)
        x = (px[:, :hk] + px[:, hk:]
             + jnp.dot(m_lo, wg_ref[:, :hk], preferred_element_type=F32)) + bg_ref[...]
        log_a = (jnp.minimum(x, 0.0) - jnp.log1p(jnp.exp(-jnp.abs(x)))) / GATE_TAU
        la1 = log_a.astype(BF16)
        res = log_a - la1.astype(F32)
        la2 = res.astype(BF16)
        las.append((la1, la2, (res - la2.astype(F32)).astype(BF16)))

    r = lax.broadcasted_iota(I32, (CHUNK, CHUNK), 0)
    c = lax.broadcasted_iota(I32, (CHUNK, CHUNK), 1)
    causal = r >= c
    tril = jnp.where(causal, 1.0, 0.0).astype(BF16)
    gn = gn_ref[...]
    contract_last = (((1,), (1,)), ((), ()))
    contract_first = (((0,), (0,)), ((), ()))

    units = [(bb, h) for bb in range(nbb) for h in range(GLA_HEADS)]
    for ci in range(tb // CHUNK):
        rs = slice(ci * CHUNK, (ci + 1) * CHUNK)
        g_all = [sum(jnp.dot(tril, la[rs], preferred_element_type=F32) for la in las[bb])
                 for bb in range(nbb)]
        qe, kd, kdl, eg_last = {}, {}, {}, {}
        for u in units:
            bb, h = u
            ks = slice(h * dk, (h + 1) * dk)
            g = g_all[bb][:, ks]
            g_last = g[CHUNK - 1:CHUNK, :]
            kh = k_ref[bb, rs, ks].astype(F32)
            qe[u] = ((q_ref[bb, rs, ks].astype(F32) * (dk ** -0.5)) * jnp.exp(g)).astype(BF16)
            kd[u] = (kh * jnp.exp(-g)).astype(BF16)
            kdl[u] = (kh * jnp.exp(g_last - g)).astype(BF16)
            eg_last[u] = jnp.exp(g_last)
        att = {}
        for u in units:
            a = lax.dot_general(qe[u], kd[u], contract_last, preferred_element_type=F32)
            att[u] = jnp.where(causal, a, 0.0).astype(BF16)
        outs = {}
        for u in units:
            bb, h = u
            vh = v_ref[bb, rs, h * dv:(h + 1) * dv]
            st = st_ref[bb, h]
            outs[u] = (jnp.dot(att[u], vh, preferred_element_type=F32)
                       + lax.dot_general(qe[u], st.astype(BF16), contract_last,
                                         preferred_element_type=F32))
            upd = lax.dot_general(vh, kdl[u], contract_first, preferred_element_type=F32)
            st_ref[bb, h] = st * eg_last[u] + upd
        for u in units:
            bb, h = u
            vs = slice(h * dv, (h + 1) * dv)
            o = outs[u]
            y = o * lax.rsqrt(jnp.mean(o * o, axis=-1, keepdims=True) + EPS) * gn
            gate = gb_ref[bb, rs, vs].astype(F32)
            o_ref[bb, rs, vs] = (y * (gate * jax.nn.sigmoid(gate))).astype(BF16)


def _gla(qb, kb, vb, gb, misc, wg_pad, b_gate, gla_norm_g, tb):
    B, T, HK = qb.shape
    HV = vb.shape[2]
    dk, dv = HK // GLA_HEADS, HV // GLA_HEADS
    nbb = 2 if B % 2 == 0 else 1
    blk = lambda b, i: (b, i, 0)
    const = lambda b, i: (0, 0)
    return pl.pallas_call(
        functools.partial(_gla_kernel, tb=tb, dk=dk, dv=dv),
        out_shape=jax.ShapeDtypeStruct((B, T, HV), BF16),
        grid=(B // nbb, T // tb),
        in_specs=[pl.BlockSpec((nbb, tb, HK), blk),
                  pl.BlockSpec((nbb, tb, HK), blk),
                  pl.BlockSpec((nbb, tb, HV), blk),
                  pl.BlockSpec((nbb, tb, HV), blk),
                  pl.BlockSpec((nbb, tb, MISC_W), blk),
                  pl.BlockSpec((MISC_W, 2 * HK), const),
                  pl.BlockSpec((1, HK), const),
                  pl.BlockSpec((1, dv), const)],
        out_specs=pl.BlockSpec((nbb, tb, HV), blk),
        scratch_shapes=[pltpu.VMEM((nbb, GLA_HEADS, dv, dk), F32)],
        compiler_params=_params(2),
        name="gla",
    )(qb, kb, vb, gb, misc, wg_pad, b_gate.reshape(1, -1), gla_norm_g.reshape(1, -1))


def _pack_bf16_pairs(x):
    half = x.shape[1] // 2
    bits = lax.bitcast_convert_type(x.astype(BF16).astype(F32), I32)
    return (bits[:, half:] & jnp.int32(-65536)) | lax.shift_right_logical(bits[:, :half], 16)


def _unpack_bf16_pairs(w):
    lo = lax.bitcast_convert_type(w << 16, F32)
    hi = lax.bitcast_convert_type(w & jnp.int32(-65536), F32)
    return jnp.concatenate([lo, hi], axis=1)


def _merge_kernel(ya_ref, yb_ref, ga_ref, gbm_ref, x_ref, gt_ref, sc_ref, sh_ref, g2_ref,
                  wa_ref, wb_ref, wo_ref, rw_ref, rb_ref,
                  x1_ref, h2_ref, e_ref, w_ref, cnt_ref, *, n_exp):
    pa = jnp.dot(ya_ref[0], wa_ref[...], preferred_element_type=F32)
    pb = jnp.dot(yb_ref[0], wb_ref[...], preferred_element_type=F32)
    merged = (jax.nn.sigmoid(ga_ref[0].astype(F32)) * pa
              + jax.nn.sigmoid(gbm_ref[0].astype(F32)) * pb)
    mo = jnp.dot(merged.astype(BF16), wo_ref[...], preferred_element_type=F32)
    x1 = x_ref[0] + gt_ref[0] * mo
    x1_ref[0] = x1
    h2 = _rms_mod(x1, g2_ref[...], sc_ref[0], sh_ref[0])
    h2_ref[0] = _pack_bf16_pairs(h2)

    h_hi = h2.astype(BF16)
    h_lo = (h2 - h_hi.astype(F32)).astype(BF16)
    part = jnp.dot(h_hi, rw_ref[...], preferred_element_type=F32)
    logits = (part[:, :LANES] + part[:, LANES:]
              + jnp.dot(h_lo, rw_ref[:, :LANES], preferred_element_type=F32)) + rb_ref[...]
    tm = logits.shape[0]
    lane = lax.broadcasted_iota(I32, (tm, LANES), 1)
    logits = jnp.where(lane < n_exp, logits, -jnp.inf)
    top_v, top_e = [], []
    for _ in range(TOP_K):
        mx = jnp.max(logits, axis=1, keepdims=True)
        idx = jnp.min(jnp.where(logits == mx, lane, LANES), axis=1, keepdims=True)
        logits = jnp.where(lane == idx, -jnp.inf, logits)
        top_v.append(mx)
        top_e.append(idx)
    ex = [jnp.exp(v - top_v[0]) for v in top_v]
    den = ex[0]
    for t in ex[1:]:
        den = den + t
    e_out = jnp.zeros((tm, LANES), I32)
    w_out = jnp.zeros((tm, LANES), F32)
    hot = jnp.zeros((tm, LANES), F32)
    for k in range(TOP_K):
        e_out = jnp.where(lane == k, top_e[k], e_out)
        w_out = jnp.where(lane == k, ex[k] / den, w_out)
        hot = hot + jnp.where(lane == top_e[k], 1.0, 0.0)
    e_ref[0] = e_out
    w_ref[0] = w_out

    @pl.when(jnp.logical_and(pl.program_id(0) == 0, pl.program_id(1) == 0))
    def _():
        cnt_ref[...] = jnp.zeros(cnt_ref.shape, F32)

    cnt_ref[0:1, :] += jnp.sum(hot, axis=0, keepdims=True)


def _merge(ya, yb, ga, gbm, x, gt1, sc2, sh2, g2, wa, wb, wo, rw_pad, rb_pad, n_exp, tm):
    B, T, D = x.shape
    row = lambda b, i: (b, i, 0)
    per_b = lambda b, i: (b, 0, 0)
    const = lambda b, i: (0, 0)
    resident = functools.partial(pl.BlockSpec, index_map=const, pipeline_mode=pl.Buffered(1))
    return pl.pallas_call(
        functools.partial(_merge_kernel, n_exp=n_exp),
        out_shape=[jax.ShapeDtypeStruct((B, T, D), F32),
                   jax.ShapeDtypeStruct((B, T, D // 2), I32),
                   jax.ShapeDtypeStruct((B, T, LANES), I32),
                   jax.ShapeDtypeStruct((B, T, LANES), F32),
                   jax.ShapeDtypeStruct((8, LANES), F32)],
        grid=(B, T // tm),
        in_specs=[pl.BlockSpec((1, tm, D), row),
                  pl.BlockSpec((1, tm, D), row),
                  pl.BlockSpec((1, tm, D), row),
                  pl.BlockSpec((1, tm, D), row),
                  pl.BlockSpec((1, tm, D), row),
                  pl.BlockSpec((1, 1, D), per_b),
                  pl.BlockSpec((1, 1, D), per_b),
                  pl.BlockSpec((1, 1, D), per_b),
                  pl.BlockSpec((1, D), const),
                  resident((D, D)), resident((D, D)), resident((D, D)),
                  resident((D, 2 * LANES)), pl.BlockSpec((1, LANES), const)],
        out_specs=[pl.BlockSpec((1, tm, D), row),
                   pl.BlockSpec((1, tm, D // 2), row),
                   pl.BlockSpec((1, tm, LANES), row),
                   pl.BlockSpec((1, tm, LANES), row),
                   pl.BlockSpec((8, LANES), const)],
        compiler_params=_params(2),
        name="merge",
    )(ya, yb, ga, gbm, x, gt1, sc2, sh2, g2, wa, wb, wo, rw_pad, rb_pad)


def _plan_kernel(e_ref, cnt_ref, slot_ref, blk_ref, nblk_ref, base_ref, *, n_exp, tmb, nbp):
    tt = e_ref.shape[0]
    lane1 = lax.broadcasted_iota(I32, (1, LANES), 1)

    @pl.when(pl.program_id(0) == 0)
    def _():
        cnt = cnt_ref[0:1, :]
        padded = jnp.where(lane1 < n_exp, jnp.ceil(cnt / tmb) * tmb, 0.0)
        r = lax.broadcasted_iota(I32, (LANES, LANES), 0)
        c = lax.broadcasted_iota(I32, (LANES, LANES), 1)
        upper = jnp.where(r < c, 1.0, 0.0)
        start = jnp.dot(jnp.broadcast_to(padded, (8, LANES)), upper, precision=HI,
                        preferred_element_type=F32)[0:1, :]
        base_ref[...] = start
        end = start + padded
        jrow = lax.broadcasted_iota(I32, (nbp, LANES), 0).astype(F32) * tmb
        owner = jnp.sum(jnp.where(jnp.logical_and(end <= jrow, lane1 < n_exp), 1.0, 0.0),
                        axis=1, keepdims=True)
        blk_ref[...] = jnp.broadcast_to(jnp.minimum(owner, n_exp - 1.0), (nbp, LANES)).astype(I32)
        total = jnp.sum(padded, axis=1, keepdims=True)
        nblk_ref[...] = jnp.broadcast_to(total / tmb, (8, LANES)).astype(I32)

    e = e_ref[...]
    lane = lax.broadcasted_iota(I32, (tt, LANES), 1)
    hot = jnp.zeros((tt, LANES), F32)
    for k in range(TOP_K):
        hot = hot + jnp.where(lane == e[:, k:k + 1], 1.0, 0.0)
    r = lax.broadcasted_iota(I32, (tt, tt), 0)
    c = lax.broadcasted_iota(I32, (tt, tt), 1)
    lower = jnp.where(r > c, 1.0, 0.0).astype(BF16)
    rank = jnp.dot(lower, hot.astype(BF16), preferred_element_type=F32)
    pos = base_ref[...] + rank
    out = jnp.zeros((tt, LANES), F32)
    for k in range(TOP_K):
        sk = jnp.sum(jnp.where(lane == e[:, k:k + 1], pos, 0.0), axis=1, keepdims=True)
        out = jnp.where(lane == k, sk, out)
    slot_ref[...] = out.astype(I32)
    base_ref[...] += jnp.sum(hot, axis=0, keepdims=True)


def _plan(e_pad, cnt, n_exp, tmb, nbp, tt):
    N = e_pad.shape[0]
    return pl.pallas_call(
        functools.partial(_plan_kernel, n_exp=n_exp, tmb=tmb, nbp=nbp),
        out_shape=[jax.ShapeDtypeStruct((N, LANES), I32),
                   jax.ShapeDtypeStruct((nbp, LANES), I32),
                   jax.ShapeDtypeStruct((8, LANES), I32)],
        grid=(N // tt,),
        in_specs=[pl.BlockSpec((tt, LANES), lambda i: (i, 0)),
                  pl.BlockSpec((8, LANES), lambda i: (0, 0))],
        out_specs=[pl.BlockSpec((tt, LANES), lambda i: (i, 0)),
                   pl.BlockSpec((nbp, LANES), lambda i: (0, 0)),
                   pl.BlockSpec((8, LANES), lambda i: (0, 0))],
        scratch_shapes=[pltpu.VMEM((1, LANES), F32)],
        compiler_params=_params(1),
        name="moe_plan",
    )(e_pad, cnt)


def _dispatch_kernel(slot_ref, h_ref, xs_in_ref, xs_ref, sem, *, tt):
    del xs_in_ref

    def start(g, carry):
        t0 = pl.multiple_of(g * ROW_GROUP, ROW_GROUP)
        for j in range(ROW_GROUP):
            for k in range(TOP_K):
                s = slot_ref[(t0 + j) * TOP_K + k]
                pltpu.make_async_copy(h_ref.at[pl.ds(t0 + j, 1)], xs_ref.at[pl.ds(s, 1)],
                                      sem).start(priority=k % 2)
        return carry

    lax.fori_loop(0, tt // ROW_GROUP, start, 0)
    for k in range(TOP_K):
        pltpu.make_async_copy(h_ref, xs_ref.at[pl.ds(0, tt)], sem).wait()


def _dispatch(slots_flat, h2, xs0, tt):
    N, D = h2.shape
    return pl.pallas_call(
        functools.partial(_dispatch_kernel, tt=tt),
        out_shape=jax.ShapeDtypeStruct(xs0.shape, xs0.dtype),
        grid=(N // tt,),
        in_specs=[pl.BlockSpec((tt * TOP_K,), lambda i: (i,), memory_space=pltpu.SMEM),
                  pl.BlockSpec((tt, D), lambda i: (i, 0)),
                  pl.BlockSpec(memory_space=pl.ANY)],
        out_specs=pl.BlockSpec(memory_space=pl.ANY),
        scratch_shapes=[pltpu.SemaphoreType.DMA],
        input_output_aliases={2: 0},
        compiler_params=_params(1),
        name="moe_dispatch",
    )(slots_flat, h2, xs0)


def _experts_kernel(blk_ref, nblk_ref, xs_ref, wgu_ref, bgu_ref, wd_ref, bd_ref, ys_ref,
                    wgu_bf, wd_bf, *, ff):
    j = pl.program_id(0)
    e = blk_ref[j]
    fresh = jnp.logical_or(j == 0, e != blk_ref[jnp.maximum(j - 1, 0)])
    active = j < nblk_ref[0]

    @pl.when(jnp.logical_and(active, fresh))
    def _():
        wgu_bf[...] = wgu_ref[0].astype(BF16)
        wd_bf[...] = wd_ref[0].astype(BF16)

    @pl.when(active)
    def _():
        x = _unpack_bf16_pairs(xs_ref[...]).astype(BF16)
        hgu = jnp.dot(x, wgu_bf[...], preferred_element_type=F32) + bgu_ref[0]
        gate = jnp.minimum(hgu[:, :ff], SWIGLU_LIMIT)
        up = jnp.clip(hgu[:, ff:], -SWIGLU_LIMIT, SWIGLU_LIMIT)
        act = (up + 1.0) * (gate * jax.nn.sigmoid(SWIGLU_ALPHA * gate))
        y = jnp.dot(act.astype(BF16), wd_bf[...], preferred_element_type=F32) + bd_ref[0]
        ys_ref[...] = _pack_bf16_pairs(y)

    @pl.when(jnp.logical_not(active))
    def _():
        ys_ref[...] = jnp.zeros(ys_ref.shape, I32)


def _experts(blk_e, nblk, xs, w_gate_up, b_gate_up, w_down, b_down, tmb):
    P, dp = xs.shape
    E, D, ff2 = w_gate_up.shape
    ff = ff2 // 2
    by_expert = lambda j, blk, nb: (blk[j], 0, 0)
    return pl.pallas_call(
        functools.partial(_experts_kernel, ff=ff),
        out_shape=jax.ShapeDtypeStruct((P, dp), I32),
        grid_spec=pltpu.PrefetchScalarGridSpec(
            num_scalar_prefetch=2,
            grid=(P // tmb,),
            in_specs=[pl.BlockSpec((tmb, dp), lambda j, blk, nb: (j, 0)),
                      pl.BlockSpec((1, D, ff2), by_expert),
                      pl.BlockSpec((1, 1, ff2), by_expert),
                      pl.BlockSpec((1, ff, D), by_expert),
                      pl.BlockSpec((1, 1, D), by_expert)],
            out_specs=pl.BlockSpec((tmb, dp), lambda j, blk, nb: (j, 0)),
            scratch_shapes=[pltpu.VMEM((D, ff2), BF16), pltpu.VMEM((ff, D), BF16)]),
        compiler_params=_params(1),
        name="moe_experts",
    )(blk_e, nblk, xs, w_gate_up, b_gate_up.reshape(E, 1, ff2), w_down, b_down.reshape(E, 1, D))


def _combine_kernel(slot_ref, ys_ref, w_ref, x1_ref, gt_ref, fg_ref, o_ref, buf, sem, *, tt, nt):
    step = pl.program_id(0) * nt + pl.program_id(1)
    nsteps = pl.num_programs(0) * nt

    def issue(s):
        b = s & 1

        def body(g, carry):
            t0 = pl.multiple_of(g * ROW_GROUP, ROW_GROUP)
            for j in range(ROW_GROUP):
                for k in range(TOP_K):
                    idx = slot_ref[(s * tt + t0 + j) * TOP_K + k]
                    pltpu.make_async_copy(ys_ref.at[pl.ds(idx, 1)], buf.at[b, k, pl.ds(t0 + j, 1)],
                                          sem.at[b]).start(priority=k % 2)
            return carry
        lax.fori_loop(0, tt // ROW_GROUP, body, 0)

    @pl.when(step == 0)
    def _():
        issue(step)

    @pl.when(step + 1 < nsteps)
    def _():
        issue(step + 1)

    cur = step & 1
    for k in range(TOP_K):
        pltpu.make_async_copy(ys_ref.at[pl.ds(0, tt)], buf.at[cur, k], sem.at[cur]).wait()

    w = w_ref[0]
    moe = _unpack_bf16_pairs(buf[cur, 0]) * w[:, 0:1]
    for k in range(1, TOP_K):
        moe = moe + _unpack_bf16_pairs(buf[cur, k]) * w[:, k:k + 1]
    x2 = x1_ref[0] + gt_ref[0] * moe
    y = x2 * lax.rsqrt(jnp.mean(x2 * x2, axis=-1, keepdims=True) + EPS)
    o_ref[0] = y * fg_ref[...]


def _combine(slots_flat, ys, w_pad, x1, gt2, final_g, tt):
    B, T, D = x1.shape
    nt = T // tt
    row = lambda b, i, slots: (b, i, 0)
    return pl.pallas_call(
        functools.partial(_combine_kernel, tt=tt, nt=nt),
        out_shape=jax.ShapeDtypeStruct((B, T, D), F32),
        grid_spec=pltpu.PrefetchScalarGridSpec(
            num_scalar_prefetch=1,
            grid=(B, nt),
            in_specs=[pl.BlockSpec(memory_space=pl.ANY),
                      pl.BlockSpec((1, tt, LANES), row),
                      pl.BlockSpec((1, tt, D), row),
                      pl.BlockSpec((1, 1, D), lambda b, i, slots: (b, 0, 0)),
                      pl.BlockSpec((1, D), lambda b, i, slots: (0, 0))],
            out_specs=pl.BlockSpec((1, tt, D), row),
            scratch_shapes=[pltpu.VMEM((2, TOP_K, tt, ys.shape[1]), ys.dtype),
                            pltpu.SemaphoreType.DMA((2,))]),
        compiler_params=_params(2),
        name="moe_combine",
    )(slots_flat, ys, w_pad, x1, gt2, final_g.reshape(1, -1))


def _layer(x, c, w_ada, b_ada, norm1_g, w_in, w_gate_lr, b_gate, gla_norm_g, w_branch_a,
           w_branch_b, w_out, norm2_g, router_w, router_b, w_gate_up, b_gate_up, w_down, b_down,
           final_g):
    B, T, D = x.shape
    N = B * T
    n_exp = router_w.shape[1]
    hk = w_gate_lr.shape[1]
    hv = w_branch_b.shape[0]
    ni = IDX_HEADS * IDX_DIM

    c8 = jnp.zeros((8, D), F32).at[:B].set(c)
    mod = _adaln(c8, w_ada, b_ada)[:B]
    sh1, sc1, gt1, sh2, sc2, gt2 = [m.reshape(B, 1, D) for m in jnp.split(mod, 6, axis=-1)]

    sizes = (D, D, D, ni, IDX_DIM, IDX_HEADS, hk, hk, hv, hv, GATE_RANK, D, D)
    offs = [0]
    for s in sizes:
        offs.append(offs[-1] + s)
    seg = lambda j: w_in[:, offs[j]:offs[j + 1]]
    groups = [seg(1), seg(6), seg(7), seg(8), seg(9), seg(11), seg(12)]
    widths = tuple(g.shape[1] for g in groups)
    w_main = jnp.concatenate(groups, axis=1).astype(BF16)
    t_groups = [seg(0), seg(3), seg(2)]
    t_heights = tuple(g.shape[1] for g in t_groups)
    t_out_heights = (t_heights[0], IDX_HEADS * LANES, t_heights[2])
    t_scales = ((D // DSA_HEADS) ** -0.5 * 1.4426950408889634, 1.0, 1.0)
    w_t = jnp.concatenate(t_groups, axis=1).T.astype(BF16)
    w_misc = jnp.concatenate(
        [seg(4), seg(5), seg(10), jnp.zeros((D, MISC_W - IDX_DIM - IDX_HEADS - GATE_RANK), F32)],
        axis=1)
    ka, qb, kb, vb, gb, ga, gbm, qa_t, qi_t, va_t, misc, misc_t = _inproj(
        x, norm1_g.reshape(1, D), sc1, sh1, w_main, w_t, _split_bf16(w_misc), widths, t_heights,
        t_out_heights, t_scales, min(256, T))

    ya = _dsa(qa_t, ka, va_t, qi_t, misc, misc_t, min(512, T))

    wg_pad = jnp.zeros((MISC_W, hk), F32).at[LR_OFF:LR_OFF + GATE_RANK].set(w_gate_lr)
    yb = _gla(qb, kb, vb, gb, misc, _split_bf16(wg_pad), b_gate, gla_norm_g, min(256, T))

    rw_pad = _split_bf16(jnp.zeros((D, LANES), F32).at[:, :n_exp].set(router_w))
    rb_pad = jnp.zeros((1, LANES), F32).at[0, :n_exp].set(router_b)
    x1, h2, e_pad, w_pad, cnt = _merge(
        ya, yb, ga, gbm, x, gt1, sc2, sh2, norm2_g.reshape(1, D),
        w_branch_a.astype(BF16), w_branch_b.astype(BF16), w_out.astype(BF16),
        rw_pad, rb_pad, n_exp, min(512, T))

    tmb = 256
    nb = -(-N * TOP_K // tmb) + n_exp
    nbp = -(-nb // 8) * 8
    slots, blk_e, nblk = _plan(e_pad.reshape(N, LANES), cnt, n_exp, tmb, nbp, min(512, N))

    tt_d = min(256, T)
    slots_flat = slots[:, :TOP_K].reshape(-1)
    xs = _dispatch(slots_flat, h2.reshape(N, D // 2), jnp.zeros((nb * tmb, D // 2), I32), tt_d)
    ys = _experts(blk_e[:nb, 0], nblk[0, :1], xs, w_gate_up, b_gate_up, w_down, b_down, tmb)
    tt_c = min(256, T)
    return _combine(slots_flat, ys, w_pad, x1, gt2, final_g, tt_c)


def kernel(x, c, w_ada, b_ada, norm1_g, w_in, w_gate_lr, b_gate, gla_norm_g, w_branch_a, w_branch_b,
           w_out, norm2_g, router_w, router_b, w_gate_up, b_gate_up, w_down, b_down, final_g):
    assert w_ada.shape[0] == 1, "single-layer block"
    return _layer(x, c, w_ada[0], b_ada[0], norm1_g[0], w_in[0], w_gate_lr[0], b_gate[0],
                  gla_norm_g[0], w_branch_a[0], w_branch_b[0], w_out[0], norm2_g[0], router_w[0],
                  router_b[0], w_gate_up[0], b_gate_up[0], w_down[0], b_down[0], final_g)
```

```python
import functools

import numpy as np
import jax
import jax.numpy as jnp
from jax import lax
from jax.experimental import pallas as pl
from jax.experimental.pallas import tpu as pltpu

CHUNK = 64
Q_BLOCK = 128
DSA_HEADS = 8
IDX_HEADS = 8
IDX_DIM = 64
IDX_TOPK_MAX = 256
GLA_HEADS = 4
GATE_RANK = 16
GATE_TAU = 16.0
TOP_K = 4
SWIGLU_LIMIT = 7.0
SWIGLU_ALPHA = 1.702
EPS = 1e-6

LANES = 128
ATT_UNROLL = 4
ROW_GROUP = 8
MISC_W = LANES
KI_OFF, WI_OFF, LR_OFF = 0, IDX_DIM, IDX_DIM + IDX_HEADS
NEG_SCORE = -3.0e38
NEG_SCORE_NEXT = float(np.nextafter(np.float32(NEG_SCORE), np.float32(0.0)))
NEG_BIAS = -1.0e30
VMEM_LIMIT = 56 * 1024 * 1024

F32 = jnp.float32
BF16 = jnp.bfloat16
I32 = jnp.int32
HI = lax.Precision.HIGHEST


def _params(n_axes, vmem=VMEM_LIMIT):
    return pltpu.CompilerParams(dimension_semantics=("arbitrary",) * n_axes,
                                vmem_limit_bytes=vmem)


def _log2(n):
    b = n.bit_length() - 1
    assert (1 << b) == n, n
    return b


def _adaln_kernel(c_ref, w_ref, b_ref, o_ref):
    c = c_ref[...]
    s = c * jax.nn.sigmoid(c)
    o_ref[...] = jnp.dot(s, w_ref[...], precision=HI, preferred_element_type=F32) + b_ref[...]


def _adaln(c8, w_ada, b_ada):
    D = c8.shape[1]
    n = w_ada.shape[1] // D
    return pl.pallas_call(
        _adaln_kernel,
        out_shape=jax.ShapeDtypeStruct((c8.shape[0], n * D), F32),
        grid=(n,),
        in_specs=[pl.BlockSpec((c8.shape[0], D), lambda j: (0, 0)),
                  pl.BlockSpec((D, D), lambda j: (0, j)),
                  pl.BlockSpec((1, D), lambda j: (0, j))],
        out_specs=pl.BlockSpec((c8.shape[0], D), lambda j: (0, j)),
        compiler_params=_params(1),
        name="adaln",
    )(c8, w_ada, b_ada.reshape(1, -1))


def _rms_mod(x, g, sc, sh):
    y = x * lax.rsqrt(jnp.mean(x * x, axis=-1, keepdims=True) + EPS)
    return (y * g) * (1.0 + sc) + sh


_CONTRACT_LAST = (((1,), (1,)), ((), ()))
_CONTRACT_FIRST = (((0,), (0,)), ((), ()))


def _inproj_kernel(x_ref, g_ref, sc_ref, sh_ref, *refs, n_tok, t_scales):
    n_fm = len(t_scales)
    w_refs, wt_refs = refs[:n_tok], refs[n_tok:n_tok + n_fm]
    wm_ref, wmt_ref = refs[n_tok + n_fm:n_tok + n_fm + 2]
    out_refs = refs[n_tok + n_fm + 2:]
    h = _rms_mod(x_ref[0], g_ref[...], sc_ref[0], sh_ref[0])
    hb = h.astype(BF16)
    ht = h.T
    htb = ht.astype(BF16)
    h_lo = (h - hb.astype(F32)).astype(BF16)
    ht_lo = (ht - htb.astype(F32)).astype(BF16)
    pt = jnp.dot(wmt_ref[...], htb, preferred_element_type=F32)
    out_refs[-1][0] = (pt[:MISC_W] + pt[MISC_W:]
                       + jnp.dot(wmt_ref[:MISC_W, :], ht_lo, preferred_element_type=F32))
    pm = jnp.dot(hb, wm_ref[...], preferred_element_type=F32)
    out_refs[-2][0] = (pm[:, :MISC_W] + pm[:, MISC_W:]
                       + jnp.dot(h_lo, wm_ref[:, :MISC_W], preferred_element_type=F32))
    for o_ref, w_ref in zip(out_refs[:n_tok], w_refs):
        o_ref[0] = jnp.dot(hb, w_ref[...], preferred_element_type=F32).astype(BF16)
    for o_ref, wt_ref, scale in zip(out_refs[n_tok:-2], wt_refs, t_scales):
        hgt = wt_ref.shape[0]
        val = jnp.dot(wt_ref[...], htb, preferred_element_type=F32)
        val = (val if scale == 1.0 else val * scale).astype(BF16)
        if len(o_ref.shape) == 4:
            o_ref[0, 0] = val
        elif o_ref.shape[1] == hgt:
            o_ref[0] = val
        else:
            for j in range(hgt // IDX_DIM):
                o_ref[0, j * LANES:j * LANES + IDX_DIM, :] = val[j * IDX_DIM:(j + 1) * IDX_DIM, :]
                o_ref[0, j * LANES + IDX_DIM:(j + 1) * LANES, :] = jnp.zeros(
                    (LANES - IDX_DIM, val.shape[1]), BF16)


def _inproj(x, g, sc, sh, w_tok, w_fm, w_misc, t_out_heights, t_scales, tm):
    B, T, D = x.shape
    row = lambda b, i: (b, i, 0)
    col = lambda b, i: (b, 0, i)
    per_b = lambda b, i: (b, 0, 0)
    const = lambda b, i: (0, 0)
    resident = functools.partial(pl.BlockSpec, index_map=const, pipeline_mode=pl.Buffered(1))
    weights = list(w_tok) + list(w_fm) + [w_misc, w_misc.T]
    out_shape = [jax.ShapeDtypeStruct((B, T, w.shape[1]), BF16) for w in w_tok]
    out_specs = [pl.BlockSpec((1, tm, w.shape[1]), row) for w in w_tok]
    out_shape += [jax.ShapeDtypeStruct((B, hgt, T), BF16) for hgt in t_out_heights[:-1]]
    out_specs += [pl.BlockSpec((1, hgt, tm), col) for hgt in t_out_heights[:-1]]
    out_shape += [jax.ShapeDtypeStruct((B, T // tm, t_out_heights[-1], tm), BF16),
                  jax.ShapeDtypeStruct((B, T, MISC_W), F32),
                  jax.ShapeDtypeStruct((B, MISC_W, T), F32)]
    out_specs += [pl.BlockSpec((1, 1, t_out_heights[-1], tm), lambda b, i: (b, i, 0, 0)),
                  pl.BlockSpec((1, tm, MISC_W), row),
                  pl.BlockSpec((1, MISC_W, tm), col)]
    return pl.pallas_call(
        functools.partial(_inproj_kernel, n_tok=len(w_tok), t_scales=t_scales),
        out_shape=out_shape,
        grid=(B, T // tm),
        in_specs=[pl.BlockSpec((1, tm, D), row),
                  pl.BlockSpec((1, D), const),
                  pl.BlockSpec((1, 1, D), per_b),
                  pl.BlockSpec((1, 1, D), per_b)] + [resident(w.shape) for w in weights],
        out_specs=out_specs,
        compiler_params=_params(2),
        name="inproj",
    )(x, g, sc, sh, *weights)


def _split_bf16(w):
    hi = w.astype(BF16)
    return jnp.concatenate([hi, (w - hi.astype(F32)).astype(BF16)], axis=-1)


def _dsa_kernel(q_ref, k_ref, vt_ref, qi_ref, mall_ref, mt_ref, o_ref, sc_ref, lg0_ref, lg1_ref,
                acc_ref, *, T, DQ, KT, KC, topk, dh):
    i = pl.program_id(1)
    lg_refs = (lg0_ref, lg1_ref)
    nt = ((i + 1) * DQ + KT - 1) // KT
    qpos = i * DQ + lax.broadcasted_iota(I32, (1, DQ), 1)
    lim = ((qpos >> _log2(CHUNK)) + 1) << _log2(CHUNK)
    krow = lax.broadcasted_iota(I32, (KT, DQ), 0)

    def tile(kt):
        return pl.ds(pl.multiple_of(kt * KT, KT), KT)

    def half_tile(kt):
        return pl.ds(pl.multiple_of(kt * (KT // 2), KT // 2), KT // 2)

    def trunc_bf16(x):
        bits = lax.bitcast_convert_type(x, I32) & jnp.int32(-65536)
        return lax.bitcast_convert_type(bits, F32).astype(BF16)

    wi_t = mt_ref[0, WI_OFF:WI_OFF + IDX_HEADS, :] * ((IDX_HEADS ** -0.5) * (IDX_DIM ** -0.5))

    def score_tile(kt, carry):
        ki_t = mall_ref[0, tile(kt), :].astype(BF16)
        s = jnp.zeros((KT, DQ), F32)
        for h in range(IDX_HEADS):
            x = jnp.dot(ki_t, qi_ref[0, h * LANES:(h + 1) * LANES, :], preferred_element_type=F32)
            s = s + jnp.maximum(x, 0.0) * wi_t[h:h + 1, :]
        s = jnp.where(krow + kt * KT < lim, s, NEG_SCORE)
        sc_ref[tile(kt), :] = s
        lg1_ref[half_tile(kt), :] = pltpu.bitcast(trunc_bf16(s), F32)
        return carry

    lax.fori_loop(0, nt, score_tile, 0)

    def count(pred):
        def body(kt, acc):
            m = jnp.where(pred(sc_ref[tile(kt), :], kt * KT), 1.0, 0.0)
            return acc + jnp.sum(m, axis=0, keepdims=True)
        return lax.fori_loop(0, nt, body, jnp.zeros((1, DQ), F32))

    int_min = jnp.int32(-2 ** 31)

    def key_to_float(u):
        bits = jnp.where(u < 0, u ^ int_min, ~u)
        return lax.bitcast_convert_type(bits, F32)

    one_b, zero_b = jnp.ones((), BF16), jnp.zeros((), BF16)
    rows_b = 16

    def count_hi(tb):
        def body(kt, acc):
            hi = pltpu.bitcast(lg1_ref[half_tile(kt), :], BF16)
            m = jnp.where(hi >= tb, one_b, zero_b)
            for j in range(KT // rows_b):
                acc = acc + m[j * rows_b:(j + 1) * rows_b, :]
            return acc
        acc = lax.fori_loop(0, nt, body, jnp.zeros((rows_b, DQ), BF16))
        return jnp.sum(acc.astype(F32), axis=0, keepdims=True)

    def hi_body(it, carry):
        u, c_ge = carry
        trial = u | (jnp.int32(1) << (31 - it))
        cnt = count_hi(trunc_bf16(key_to_float(trial)))
        ok = cnt >= topk
        return jnp.where(ok, trial, u), jnp.where(ok, cnt, c_ge)

    def lo_body(it, carry):
        u, c_ge = carry
        trial = u | (jnp.int32(1) << (15 - it))
        t = key_to_float(trial)
        cnt = count(lambda s, c0: s >= t)
        ok = cnt >= topk
        return jnp.where(ok, trial, u), jnp.where(ok, cnt, c_ge)

    carry = (jnp.zeros((1, DQ), I32), jnp.zeros((1, DQ), F32))
    carry = lax.fori_loop(0, 16, hi_body, carry)
    u, c_ge = lax.fori_loop(0, 16, lo_body, carry)
    thr = key_to_float(u)

    c_gt = count(lambda s, c0: s > thr)
    need = topk - c_gt
    excess = jnp.logical_and(c_ge - c_gt > need, thr > NEG_SCORE)
    has_excess = jnp.max(jnp.where(excess, 1.0, 0.0)) > 0.0
    few = thr <= NEG_SCORE
    need_ties = jnp.where(few, 0.0, need)
    thr_ge = jnp.where(few, NEG_SCORE_NEXT, thr)

    def bias_no_ties():
        def body(kt, carry):
            sc_ref[tile(kt), :] = jnp.where(sc_ref[tile(kt), :] >= thr_ge, 0.0, NEG_BIAS)
            return carry
        lax.fori_loop(0, nt, body, 0)

    def bias_with_ties():
        r = lax.broadcasted_iota(I32, (KT, KT), 0)
        c = lax.broadcasted_iota(I32, (KT, KT), 1)
        before = jnp.where(c < r, 1.0, 0.0).astype(BF16)

        def body(kt, seen):
            s = sc_ref[tile(kt), :]
            eq = jnp.where(s == thr, 1.0, 0.0)
            rank = seen + jnp.dot(before, eq.astype(BF16), preferred_element_type=F32)
            keep = jnp.where(s > thr, 1.0, jnp.where(rank < need_ties, eq, 0.0))
            sc_ref[tile(kt), :] = jnp.where(keep > 0.5, 0.0, NEG_BIAS)
            return seen + jnp.sum(eq, axis=0, keepdims=True)
        lax.fori_loop(0, nt, body, jnp.zeros((1, DQ), F32))

    lax.cond(has_excess, bias_with_ties, bias_no_ties)

    def fused(h_b, h_a, kts, mx_b, den, mx_a):
        if h_a is not None:
            hs_a = slice(h_a * dh, (h_a + 1) * dh)
            lgs = [jnp.dot(k_ref[0, tile(kt), hs_a], q_ref[0, hs_a, :], preferred_element_type=F32)
                   for kt in kts]
        if h_b is not None:
            hs_b = slice(h_b * dh, (h_b + 1) * dh)
            ps = [jnp.exp2(lg_refs[h_b % 2][tile(kt), :] - mx_b) for kt in kts]
            pv = jnp.zeros((dh, DQ), F32)
            for kt, p in zip(kts, ps):
                pb = p.astype(BF16)
                for j in range(KT // KC):
                    pv = pv + jnp.dot(vt_ref[0, kt * (KT // KC) + j, hs_b, :],
                                      pb[j * KC:(j + 1) * KC, :], preferred_element_type=F32)
                den = den + jnp.sum(p, axis=0, keepdims=True)
            acc_ref[hs_b, :] += pv
        if h_a is not None:
            for kt, lg in zip(kts, lgs):
                s = lg + sc_ref[tile(kt), :]
                lg_refs[h_a % 2][tile(kt), :] = s
                mx_a = jnp.maximum(mx_a, jnp.max(s, axis=0, keepdims=True))
        return den, mx_a

    def tile_loop(h_b, h_a, mx_b):
        def run(first, count, carry):
            for i in range(count):
                carry = fused(h_b, h_a, [first + i], mx_b, *carry)
            return carry

        carry = lax.fori_loop(0, nt // ATT_UNROLL,
                              lambda j, c: run(ATT_UNROLL * j, ATT_UNROLL, c), (den0, mx0))
        done = (nt // ATT_UNROLL) * ATT_UNROLL
        width = ATT_UNROLL // 2
        while width >= 1:
            take = (nt & width) != 0
            carry = lax.cond(take, functools.partial(run, done, width), lambda c: c, carry)
            done = done + jnp.where(take, width, 0)
            width //= 2
        return carry

    mx0 = jnp.full((1, DQ), NEG_BIAS, F32)
    den0 = jnp.zeros((1, DQ), F32)
    acc_ref[...] = jnp.zeros(acc_ref.shape, F32)
    _, mx = tile_loop(None, 0, None)
    for h in range(DSA_HEADS):
        den, mx_next = tile_loop(h, h + 1 if h + 1 < DSA_HEADS else None, mx)
        hs = slice(h * dh, (h + 1) * dh)
        o_ref[0, :, hs] = (acc_ref[hs, :] / den).T.astype(BF16)
        mx = mx_next


def _dsa(qa_t, ka, va_t, qi_t, misc, misc_t, dq):
    B, T, D = ka.shape
    dh = D // DSA_HEADS
    KC = va_t.shape[3]
    KT = min(512, T)
    topk = min(IDX_TOPK_MAX, T // 4)
    assert KT >= topk and KT % KC == 0 and dq % CHUNK == 0
    assert T // 16 <= 256, "packed bf16 counters must stay exact"
    blk = lambda b, i: (b, i, 0)
    whole = functools.partial(pl.BlockSpec, pipeline_mode=pl.Buffered(1))
    return pl.pallas_call(
        functools.partial(_dsa_kernel, T=T, DQ=dq, KT=KT, KC=KC, topk=topk, dh=dh),
        out_shape=jax.ShapeDtypeStruct((B, T, D), BF16),
        grid=(B, T // dq),
        in_specs=[pl.BlockSpec((1, D, dq), lambda b, i: (b, 0, i)),
                  whole((1, T, D), lambda b, i: (b, 0, 0)),
                  whole((1, T // KC, D, KC), lambda b, i: (b, 0, 0, 0)),
                  pl.BlockSpec((1, IDX_HEADS * LANES, dq), lambda b, i: (b, 0, i)),
                  whole((1, T, MISC_W), lambda b, i: (b, 0, 0)),
                  pl.BlockSpec((1, MISC_W, dq), lambda b, i: (b, 0, i))],
        out_specs=pl.BlockSpec((1, dq, D), blk),
        scratch_shapes=[pltpu.VMEM((T, dq), F32),
                        pltpu.VMEM((T, dq), F32),
                        pltpu.VMEM((T, dq), F32),
                        pltpu.VMEM((D, dq), F32)],
        compiler_params=_params(2),
        name="dsa",
    )(qa_t, ka, va_t, qi_t, misc, misc_t)


def _gla_kernel(q_ref, k_ref, v_ref, gb_ref, misc_ref, wg_ref, bg_ref, gn_ref, o_ref, st_ref,
                *, tb, dk, dv):
    @pl.when(pl.program_id(1) == 0)
    def _():
        st_ref[...] = jnp.zeros(st_ref.shape, F32)

    hk = wg_ref.shape[1] // 2
    nbb = q_ref.shape[0]
    las = []
    for bb in range(nbb):
        m = misc_ref[bb]
        m_hi = m.astype(BF16)
        m_lo = (m - m_hi.astype(F32)).astype(BF16)
        px = jnp.dot(m_hi, wg_ref[...], preferred_element_type=F32)
        x = (px[:, :hk] + px[:, hk:]
             + jnp.dot(m_lo, wg_ref[:, :hk], preferred_element_type=F32)) + bg_ref[...]
        log_a = (jnp.minimum(x, 0.0) - jnp.log1p(jnp.exp(-jnp.abs(x)))) / GATE_TAU
        la1 = log_a.astype(BF16)
        res = log_a - la1.astype(F32)
        la2 = res.astype(BF16)
        las.append((la1, la2, (res - la2.astype(F32)).astype(BF16)))

    r = lax.broadcasted_iota(I32, (CHUNK, CHUNK), 0)
    c = lax.broadcasted_iota(I32, (CHUNK, CHUNK), 1)
    causal = r >= c
    tril = jnp.where(causal, 1.0, 0.0).astype(BF16)
    gn = gn_ref[...]
    contract_last = (((1,), (1,)), ((), ()))
    contract_first = (((0,), (0,)), ((), ()))

    units = [(bb, h) for bb in range(nbb) for h in range(GLA_HEADS)]
    for ci in range(tb // CHUNK):
        rs = slice(ci * CHUNK, (ci + 1) * CHUNK)
        g_all = [sum(jnp.dot(tril, la[rs], preferred_element_type=F32) for la in las[bb])
                 for bb in range(nbb)]
        qe, kd, kdl, eg_last = {}, {}, {}, {}
        for u in units:
            bb, h = u
            ks = slice(h * dk, (h + 1) * dk)
            g = g_all[bb][:, ks]
            g_last = g[CHUNK - 1:CHUNK, :]
            kh = k_ref[bb, rs, ks].astype(F32)
            qe[u] = ((q_ref[bb, rs, ks].astype(F32) * (dk ** -0.5)) * jnp.exp(g)).astype(BF16)
            kd[u] = (kh * jnp.exp(-g)).astype(BF16)
            kdl[u] = (kh * jnp.exp(g_last - g)).astype(BF16)
            eg_last[u] = jnp.exp(g_last)
        att = {}
        for u in units:
            a = lax.dot_general(qe[u], kd[u], contract_last, preferred_element_type=F32)
            att[u] = jnp.where(causal, a, 0.0).astype(BF16)
        outs = {}
        for u in units:
            bb, h = u
            vh = v_ref[bb, rs, h * dv:(h + 1) * dv]
            st = st_ref[bb, h]
            outs[u] = (jnp.dot(att[u], vh, preferred_element_type=F32)
                       + lax.dot_general(qe[u], st.astype(BF16), contract_last,
                                         preferred_element_type=F32))
            upd = lax.dot_general(vh, kdl[u], contract_first, preferred_element_type=F32)
            st_ref[bb, h] = st * eg_last[u] + upd
        for u in units:
            bb, h = u
            vs = slice(h * dv, (h + 1) * dv)
            o = outs[u]
            y = o * lax.rsqrt(jnp.mean(o * o, axis=-1, keepdims=True) + EPS) * gn
            gate = gb_ref[bb, rs, vs].astype(F32)
            o_ref[bb, rs, vs] = (y * (gate * jax.nn.sigmoid(gate))).astype(BF16)


def _gla(qb, kb, vb, gb, misc, wg_pad, b_gate, gla_norm_g, tb):
    B, T, HK = qb.shape
    HV = vb.shape[2]
    dk, dv = HK // GLA_HEADS, HV // GLA_HEADS
    nbb = 2 if B % 2 == 0 else 1
    blk = lambda b, i: (b, i, 0)
    const = lambda b, i: (0, 0)
    return pl.pallas_call(
        functools.partial(_gla_kernel, tb=tb, dk=dk, dv=dv),
        out_shape=jax.ShapeDtypeStruct((B, T, HV), BF16),
        grid=(B // nbb, T // tb),
        in_specs=[pl.BlockSpec((nbb, tb, HK), blk),
                  pl.BlockSpec((nbb, tb, HK), blk),
                  pl.BlockSpec((nbb, tb, HV), blk),
                  pl.BlockSpec((nbb, tb, HV), blk),
                  pl.BlockSpec((nbb, tb, MISC_W), blk),
                  pl.BlockSpec((MISC_W, 2 * HK), const),
                  pl.BlockSpec((1, HK), const),
                  pl.BlockSpec((1, dv), const)],
        out_specs=pl.BlockSpec((nbb, tb, HV), blk),
        scratch_shapes=[pltpu.VMEM((nbb, GLA_HEADS, dv, dk), F32)],
        compiler_params=_params(2),
        name="gla",
    )(qb, kb, vb, gb, misc, wg_pad, b_gate.reshape(1, -1), gla_norm_g.reshape(1, -1))


def _pack_bf16_pairs(x):
    half = x.shape[1] // 2
    bits = lax.bitcast_convert_type(x.astype(BF16).astype(F32), I32)
    return (bits[:, half:] & jnp.int32(-65536)) | lax.shift_right_logical(bits[:, :half], 16)


def _unpack_bf16_pairs(w):
    lo = lax.bitcast_convert_type(w << 16, F32)
    hi = lax.bitcast_convert_type(w & jnp.int32(-65536), F32)
    return jnp.concatenate([lo, hi], axis=1)


def _merge_kernel(ya_ref, yb_ref, ga_ref, gbm_ref, x_ref, gt_ref, sc_ref, sh_ref, g2_ref,
                  wa_ref, wb_ref, wo_ref, rw_ref, rb_ref,
                  x1_ref, h2_ref, e_ref, w_ref, cnt_ref, *, n_exp):
    pa = jnp.dot(ya_ref[0], wa_ref[...], preferred_element_type=F32)
    pb = jnp.dot(yb_ref[0], wb_ref[...], preferred_element_type=F32)
    merged = (jax.nn.sigmoid(ga_ref[0].astype(F32)) * pa
              + jax.nn.sigmoid(gbm_ref[0].astype(F32)) * pb)
    mo = jnp.dot(merged.astype(BF16), wo_ref[...], preferred_element_type=F32)
    x1 = x_ref[0] + gt_ref[0] * mo
    x1_ref[0] = x1
    h2 = _rms_mod(x1, g2_ref[...], sc_ref[0], sh_ref[0])
    h2_ref[0] = _pack_bf16_pairs(h2)

    h_hi = h2.astype(BF16)
    h_lo = (h2 - h_hi.astype(F32)).astype(BF16)
    part = jnp.dot(h_hi, rw_ref[...], preferred_element_type=F32)
    logits = (part[:, :LANES] + part[:, LANES:]
              + jnp.dot(h_lo, rw_ref[:, :LANES], preferred_element_type=F32)) + rb_ref[...]
    tm = logits.shape[0]
    lane = lax.broadcasted_iota(I32, (tm, LANES), 1)
    logits = jnp.where(lane < n_exp, logits, -jnp.inf)
    top_v, top_e = [], []
    for _ in range(TOP_K):
        mx = jnp.max(logits, axis=1, keepdims=True)
        idx = jnp.min(jnp.where(logits == mx, lane, LANES), axis=1, keepdims=True)
        logits = jnp.where(lane == idx, -jnp.inf, logits)
        top_v.append(mx)
        top_e.append(idx)
    ex = [jnp.exp(v - top_v[0]) for v in top_v]
    den = ex[0]
    for t in ex[1:]:
        den = den + t
    e_out = jnp.zeros((tm, LANES), I32)
    w_out = jnp.zeros((tm, LANES), F32)
    hot = jnp.zeros((tm, LANES), F32)
    for k in range(TOP_K):
        e_out = jnp.where(lane == k, top_e[k], e_out)
        w_out = jnp.where(lane == k, ex[k] / den, w_out)
        hot = hot + jnp.where(lane == top_e[k], 1.0, 0.0)
    e_ref[0] = e_out
    w_ref[0] = w_out

    @pl.when(jnp.logical_and(pl.program_id(0) == 0, pl.program_id(1) == 0))
    def _():
        cnt_ref[...] = jnp.zeros(cnt_ref.shape, F32)

    cnt_ref[0:1, :] += jnp.sum(hot, axis=0, keepdims=True)


def _merge(ya, yb, ga, gbm, x, gt1, sc2, sh2, g2, wa, wb, wo, rw_pad, rb_pad, n_exp, tm):
    B, T, D = x.shape
    row = lambda b, i: (b, i, 0)
    per_b = lambda b, i: (b, 0, 0)
    const = lambda b, i: (0, 0)
    resident = functools.partial(pl.BlockSpec, index_map=const, pipeline_mode=pl.Buffered(1))
    return pl.pallas_call(
        functools.partial(_merge_kernel, n_exp=n_exp),
        out_shape=[jax.ShapeDtypeStruct((B, T, D), F32),
                   jax.ShapeDtypeStruct((B, T, D // 2), I32),
                   jax.ShapeDtypeStruct((B, T, LANES), I32),
                   jax.ShapeDtypeStruct((B, T, LANES), F32),
                   jax.ShapeDtypeStruct((8, LANES), F32)],
        grid=(B, T // tm),
        in_specs=[pl.BlockSpec((1, tm, D), row),
                  pl.BlockSpec((1, tm, D), row),
                  pl.BlockSpec((1, tm, D), row),
                  pl.BlockSpec((1, tm, D), row),
                  pl.BlockSpec((1, tm, D), row),
                  pl.BlockSpec((1, 1, D), per_b),
                  pl.BlockSpec((1, 1, D), per_b),
                  pl.BlockSpec((1, 1, D), per_b),
                  pl.BlockSpec((1, D), const),
                  resident((D, D)), resident((D, D)), resident((D, D)),
                  resident((D, 2 * LANES)), pl.BlockSpec((1, LANES), const)],
        out_specs=[pl.BlockSpec((1, tm, D), row),
                   pl.BlockSpec((1, tm, D // 2), row),
                   pl.BlockSpec((1, tm, LANES), row),
                   pl.BlockSpec((1, tm, LANES), row),
                   pl.BlockSpec((8, LANES), const)],
        compiler_params=_params(2),
        name="merge",
    )(ya, yb, ga, gbm, x, gt1, sc2, sh2, g2, wa, wb, wo, rw_pad, rb_pad)


def _plan_kernel(e_ref, cnt_ref, slot_ref, blk_ref, nblk_ref, base_ref, *, n_exp, tmb, nbp):
    tt = e_ref.shape[0]
    lane1 = lax.broadcasted_iota(I32, (1, LANES), 1)

    @pl.when(pl.program_id(0) == 0)
    def _():
        cnt = cnt_ref[0:1, :]
        padded = jnp.where(lane1 < n_exp, jnp.ceil(cnt / tmb) * tmb, 0.0)
        r = lax.broadcasted_iota(I32, (LANES, LANES), 0)
        c = lax.broadcasted_iota(I32, (LANES, LANES), 1)
        upper = jnp.where(r < c, 1.0, 0.0)
        start = jnp.dot(jnp.broadcast_to(padded, (8, LANES)), upper, precision=HI,
                        preferred_element_type=F32)[0:1, :]
        base_ref[...] = start
        end = start + padded
        jrow = lax.broadcasted_iota(I32, (nbp, LANES), 0).astype(F32) * tmb
        owner = jnp.sum(jnp.where(jnp.logical_and(end <= jrow, lane1 < n_exp), 1.0, 0.0),
                        axis=1, keepdims=True)
        total = jnp.sum(padded, axis=1, keepdims=True)
        last = jnp.sum(jnp.where(jnp.logical_and(end == jrow + tmb, padded > 0.0), 1.0, 0.0),
                       axis=1, keepdims=True)
        sparse_blk = jnp.where(jrow >= total, 1.0, last)
        lane_n = lax.broadcasted_iota(I32, (nbp, LANES), 1)
        blk_ref[...] = jnp.where(lane_n == 1, sparse_blk,
                                 jnp.minimum(owner, n_exp - 1.0)).astype(I32)
        nblk_ref[...] = jnp.broadcast_to(total / tmb, (8, LANES)).astype(I32)

    e = e_ref[...]
    lane = lax.broadcasted_iota(I32, (tt, LANES), 1)
    hot = jnp.zeros((tt, LANES), F32)
    for k in range(TOP_K):
        hot = hot + jnp.where(lane == e[:, k:k + 1], 1.0, 0.0)
    r = lax.broadcasted_iota(I32, (tt, tt), 0)
    c = lax.broadcasted_iota(I32, (tt, tt), 1)
    lower = jnp.where(r > c, 1.0, 0.0).astype(BF16)
    rank = jnp.dot(lower, hot.astype(BF16), preferred_element_type=F32)
    pos = base_ref[...] + rank
    out = jnp.zeros((tt, LANES), F32)
    for k in range(TOP_K):
        sk = jnp.sum(jnp.where(lane == e[:, k:k + 1], pos, 0.0), axis=1, keepdims=True)
        out = jnp.where(lane == k, sk, out)
    slot_ref[...] = out.astype(I32)
    base_ref[...] += jnp.sum(hot, axis=0, keepdims=True)


def _plan(e_pad, cnt, n_exp, tmb, nbp, tt):
    N = e_pad.shape[0]
    return pl.pallas_call(
        functools.partial(_plan_kernel, n_exp=n_exp, tmb=tmb, nbp=nbp),
        out_shape=[jax.ShapeDtypeStruct((N, LANES), I32),
                   jax.ShapeDtypeStruct((nbp, LANES), I32),
                   jax.ShapeDtypeStruct((8, LANES), I32)],
        grid=(N // tt,),
        in_specs=[pl.BlockSpec((tt, LANES), lambda i: (i, 0)),
                  pl.BlockSpec((8, LANES), lambda i: (0, 0))],
        out_specs=[pl.BlockSpec((tt, LANES), lambda i: (i, 0)),
                   pl.BlockSpec((nbp, LANES), lambda i: (0, 0)),
                   pl.BlockSpec((8, LANES), lambda i: (0, 0))],
        scratch_shapes=[pltpu.VMEM((1, LANES), F32)],
        compiler_params=_params(1),
        name="moe_plan",
    )(e_pad, cnt)


def _dispatch_kernel(slot_ref, sparse_ref, h_ref, xs_ref, zeros, sem, zsem, *, tt, tmb):
    @pl.when(pl.program_id(0) == 0)
    def _():
        zeros[...] = jnp.zeros(zeros.shape, zeros.dtype)

        def clear(j):
            return pltpu.make_async_copy(
                zeros, xs_ref.at[pl.ds(pl.multiple_of(j * tmb, tmb), tmb)], zsem)

        def start_clear(j, carry):
            @pl.when(sparse_ref[j] != 0)
            def _():
                clear(j).start()
            return carry

        def wait_clear(j, carry):
            @pl.when(sparse_ref[j] != 0)
            def _():
                clear(j).wait()
            return carry

        lax.fori_loop(0, sparse_ref.shape[0], start_clear, 0)
        lax.fori_loop(0, sparse_ref.shape[0], wait_clear, 0)

    def start(g, carry):
        t0 = pl.multiple_of(g * ROW_GROUP, ROW_GROUP)
        for j in range(ROW_GROUP):
            for k in range(TOP_K):
                s = slot_ref[(t0 + j) * TOP_K + k]
                pltpu.make_async_copy(h_ref.at[pl.ds(t0 + j, 1)], xs_ref.at[pl.ds(s, 1)],
                                      sem).start(priority=k % 2)
        return carry

    lax.fori_loop(0, tt // ROW_GROUP, start, 0)
    for k in range(TOP_K):
        pltpu.make_async_copy(h_ref, xs_ref.at[pl.ds(0, tt)], sem).wait()


def _dispatch(slots_flat, sparse_blk, h2, tt, tmb):
    N, D = h2.shape
    nb = sparse_blk.shape[0]
    return pl.pallas_call(
        functools.partial(_dispatch_kernel, tt=tt, tmb=tmb),
        out_shape=jax.ShapeDtypeStruct((nb * tmb, D), h2.dtype),
        grid=(N // tt,),
        in_specs=[pl.BlockSpec((tt * TOP_K,), lambda i: (i,), memory_space=pltpu.SMEM),
                  pl.BlockSpec((nb,), lambda i: (0,), memory_space=pltpu.SMEM),
                  pl.BlockSpec((tt, D), lambda i: (i, 0))],
        out_specs=pl.BlockSpec(memory_space=pl.ANY),
        scratch_shapes=[pltpu.VMEM((tmb, D), h2.dtype), pltpu.SemaphoreType.DMA,
                        pltpu.SemaphoreType.DMA],
        compiler_params=_params(1),
        name="moe_dispatch",
    )(slots_flat, sparse_blk, h2)


def _experts_kernel(blk_ref, nblk_ref, xs_ref, wgu_ref, bgu_ref, wd_ref, bd_ref, ys_ref,
                    wgu_bf, wd_bf, *, ff):
    j = pl.program_id(0)
    e = blk_ref[j]
    fresh = jnp.logical_or(j == 0, e != blk_ref[jnp.maximum(j - 1, 0)])
    active = j < nblk_ref[0]

    @pl.when(jnp.logical_and(active, fresh))
    def _():
        wgu_bf[...] = wgu_ref[0].astype(BF16)
        wd_bf[...] = wd_ref[0].astype(BF16)

    @pl.when(active)
    def _():
        x = _unpack_bf16_pairs(xs_ref[...]).astype(BF16)
        hgu = jnp.dot(x, wgu_bf[...], preferred_element_type=F32) + bgu_ref[0]
        gate = jnp.minimum(hgu[:, :ff], SWIGLU_LIMIT)
        up = jnp.clip(hgu[:, ff:], -SWIGLU_LIMIT, SWIGLU_LIMIT)
        act = (up + 1.0) * (gate * jax.nn.sigmoid(SWIGLU_ALPHA * gate))
        y = jnp.dot(act.astype(BF16), wd_bf[...], preferred_element_type=F32) + bd_ref[0]
        ys_ref[...] = _pack_bf16_pairs(y)

    @pl.when(jnp.logical_not(active))
    def _():
        ys_ref[...] = jnp.zeros(ys_ref.shape, I32)


def _experts(blk_e, nblk, xs, w_gate_up, b_gate_up, w_down, b_down, tmb):
    P, dp = xs.shape
    E, D, ff2 = w_gate_up.shape
    ff = ff2 // 2
    by_expert = lambda j, blk, nb: (blk[j], 0, 0)
    return pl.pallas_call(
        functools.partial(_experts_kernel, ff=ff),
        out_shape=jax.ShapeDtypeStruct((P, dp), I32),
        grid_spec=pltpu.PrefetchScalarGridSpec(
            num_scalar_prefetch=2,
            grid=(P // tmb,),
            in_specs=[pl.BlockSpec((tmb, dp), lambda j, blk, nb: (j, 0)),
                      pl.BlockSpec((1, D, ff2), by_expert),
                      pl.BlockSpec((1, 1, ff2), by_expert),
                      pl.BlockSpec((1, ff, D), by_expert),
                      pl.BlockSpec((1, 1, D), by_expert)],
            out_specs=pl.BlockSpec((tmb, dp), lambda j, blk, nb: (j, 0)),
            scratch_shapes=[pltpu.VMEM((D, ff2), BF16), pltpu.VMEM((ff, D), BF16)]),
        compiler_params=_params(1),
        name="moe_experts",
    )(blk_e, nblk, xs, w_gate_up, b_gate_up.reshape(E, 1, ff2), w_down, b_down.reshape(E, 1, D))


def _combine_kernel(slot_ref, ys_ref, w_ref, x1_ref, gt_ref, fg_ref, o_ref, buf, sem, *, tt, nt):
    step = pl.program_id(0) * nt + pl.program_id(1)
    nsteps = pl.num_programs(0) * nt

    def issue(s):
        b = s & 1

        def body(g, carry):
            t0 = pl.multiple_of(g * ROW_GROUP, ROW_GROUP)
            for j in range(ROW_GROUP):
                for k in range(TOP_K):
                    idx = slot_ref[(s * tt + t0 + j) * TOP_K + k]
                    pltpu.make_async_copy(ys_ref.at[pl.ds(idx, 1)], buf.at[b, k, pl.ds(t0 + j, 1)],
                                          sem.at[b]).start(priority=k % 2)
            return carry
        lax.fori_loop(0, tt // ROW_GROUP, body, 0)

    @pl.when(step == 0)
    def _():
        issue(step)

    @pl.when(step + 1 < nsteps)
    def _():
        issue(step + 1)

    cur = step & 1
    for k in range(TOP_K):
        pltpu.make_async_copy(ys_ref.at[pl.ds(0, tt)], buf.at[cur, k], sem.at[cur]).wait()

    w = w_ref[0]
    moe = _unpack_bf16_pairs(buf[cur, 0]) * w[:, 0:1]
    for k in range(1, TOP_K):
        moe = moe + _unpack_bf16_pairs(buf[cur, k]) * w[:, k:k + 1]
    x2 = x1_ref[0] + gt_ref[0] * moe
    y = x2 * lax.rsqrt(jnp.mean(x2 * x2, axis=-1, keepdims=True) + EPS)
    o_ref[0] = y * fg_ref[...]


def _combine(slots_flat, ys, w_pad, x1, gt2, final_g, tt):
    B, T, D = x1.shape
    nt = T // tt
    row = lambda b, i, slots: (b, i, 0)
    return pl.pallas_call(
        functools.partial(_combine_kernel, tt=tt, nt=nt),
        out_shape=jax.ShapeDtypeStruct((B, T, D), F32),
        grid_spec=pltpu.PrefetchScalarGridSpec(
            num_scalar_prefetch=1,
            grid=(B, nt),
            in_specs=[pl.BlockSpec(memory_space=pl.ANY),
                      pl.BlockSpec((1, tt, LANES), row),
                      pl.BlockSpec((1, tt, D), row),
                      pl.BlockSpec((1, 1, D), lambda b, i, slots: (b, 0, 0)),
                      pl.BlockSpec((1, D), lambda b, i, slots: (0, 0))],
            out_specs=pl.BlockSpec((1, tt, D), row),
            scratch_shapes=[pltpu.VMEM((2, TOP_K, tt, ys.shape[1]), ys.dtype),
                            pltpu.SemaphoreType.DMA((2,))]),
        compiler_params=_params(2),
        name="moe_combine",
    )(slots_flat, ys, w_pad, x1, gt2, final_g.reshape(1, -1))


def _layer(x, c, w_ada, b_ada, norm1_g, w_in, w_gate_lr, b_gate, gla_norm_g, w_branch_a,
           w_branch_b, w_out, norm2_g, router_w, router_b, w_gate_up, b_gate_up, w_down, b_down,
           final_g):
    B, T, D = x.shape
    N = B * T
    n_exp = router_w.shape[1]
    hk = w_gate_lr.shape[1]
    hv = w_branch_b.shape[0]
    ni = IDX_HEADS * IDX_DIM

    c8 = jnp.zeros((8, D), F32).at[:B].set(c)
    mod = _adaln(c8, w_ada, b_ada)[:B]
    sh1, sc1, gt1, sh2, sc2, gt2 = [m.reshape(B, 1, D) for m in jnp.split(mod, 6, axis=-1)]

    sizes = (D, D, D, ni, IDX_DIM, IDX_HEADS, hk, hk, hv, hv, GATE_RANK, D, D)
    offs = [0]
    for s in sizes:
        offs.append(offs[-1] + s)
    seg = lambda j: w_in[:, offs[j]:offs[j + 1]]
    w_tok = [seg(j).astype(BF16) for j in (1, 6, 7, 8, 9, 11, 12)]
    w_fm = [seg(j).T.astype(BF16) for j in (0, 3, 2)]
    t_out_heights = (sizes[0], IDX_HEADS * LANES, sizes[2])
    t_scales = ((D // DSA_HEADS) ** -0.5 * 1.4426950408889634, 1.0, 1.0)
    w_misc = jnp.concatenate(
        [seg(4), seg(5), seg(10), jnp.zeros((D, MISC_W - IDX_DIM - IDX_HEADS - GATE_RANK), F32)],
        axis=1)
    ka, qb, kb, vb, gb, ga, gbm, qa_t, qi_t, va_t, misc, misc_t = _inproj(
        x, norm1_g.reshape(1, D), sc1, sh1, w_tok, w_fm, _split_bf16(w_misc), t_out_heights,
        t_scales, min(256, T))

    ya = _dsa(qa_t, ka, va_t, qi_t, misc, misc_t, min(512, T))

    wg_pad = jnp.zeros((MISC_W, hk), F32).at[LR_OFF:LR_OFF + GATE_RANK].set(w_gate_lr)
    yb = _gla(qb, kb, vb, gb, misc, _split_bf16(wg_pad), b_gate, gla_norm_g, min(256, T))

    rw_pad = _split_bf16(jnp.zeros((D, LANES), F32).at[:, :n_exp].set(router_w))
    rb_pad = jnp.zeros((1, LANES), F32).at[0, :n_exp].set(router_b)
    x1, h2, e_pad, w_pad, cnt = _merge(
        ya, yb, ga, gbm, x, gt1, sc2, sh2, norm2_g.reshape(1, D),
        w_branch_a.astype(BF16), w_branch_b.astype(BF16), w_out.astype(BF16),
        rw_pad, rb_pad, n_exp, min(512, T))

    tmb = 256
    nb = -(-N * TOP_K // tmb) + n_exp
    nbp = -(-nb // 8) * 8
    slots, blk_e, nblk = _plan(e_pad.reshape(N, LANES), cnt, n_exp, tmb, nbp, min(512, N))

    tt_d = min(256, T)
    slots_flat = slots[:, :TOP_K].reshape(-1)
    xs = _dispatch(slots_flat, blk_e[:nb, 1], h2.reshape(N, D // 2), tt_d, tmb)
    ys = _experts(blk_e[:nb, 0], nblk[0, :1], xs, w_gate_up, b_gate_up, w_down, b_down, tmb)
    tt_c = min(256, T)
    return _combine(slots_flat, ys, w_pad, x1, gt2, final_g, tt_c)


def kernel(x, c, w_ada, b_ada, norm1_g, w_in, w_gate_lr, b_gate, gla_norm_g, w_branch_a, w_branch_b,
           w_out, norm2_g, router_w, router_b, w_gate_up, b_gate_up, w_down, b_down, final_g):
    assert w_ada.shape[0] == 1, "single-layer block"
    return _layer(x, c, w_ada[0], b_ada[0], norm1_g[0], w_in[0], w_gate_lr[0], b_gate[0],
                  gla_norm_g[0], w_branch_a[0], w_branch_b[0], w_out[0], norm2_g[0], router_w[0],
                  router_b[0], w_gate_up[0], b_gate_up[0], w_down[0], b_down[0], final_g)
```

```python
import functools
from typing import NamedTuple

import numpy as np
import jax
import jax.numpy as jnp
from jax import lax
from jax.experimental import pallas as pl
from jax.experimental.pallas import tpu as pltpu

CHUNK = 64
DSA_HEADS = 8
IDX_HEADS = 8
IDX_DIM = 64
IDX_TOPK_MAX = 256
GLA_HEADS = 4
GATE_RANK = 16
GATE_TAU = 16.0
TOP_K = 4
SWIGLU_LIMIT = 7.0
SWIGLU_ALPHA = 1.702
EPS = 1e-6

LANES = 128
ROW_GROUP = 8
BF16_ROWS = 16
ATT_UNROLL = 4
KEY_TILE = 512
KEY_BITS = 32
LOG2_E = 1.4426950408889634
MISC_W = LANES
WI_OFF, LR_OFF = IDX_DIM, IDX_DIM + IDX_HEADS
NEG_SCORE = -3.0e38
NEG_SCORE_NEXT = float(np.nextafter(np.float32(NEG_SCORE), np.float32(0.0)))
NEG_BIAS = -1.0e30
VMEM_LIMIT = 56 * 1024 * 1024

F32 = jnp.float32
BF16 = jnp.bfloat16
I32 = jnp.int32
HI = lax.Precision.HIGHEST


def _params(n_axes, vmem=VMEM_LIMIT):
    return pltpu.CompilerParams(dimension_semantics=("arbitrary",) * n_axes,
                                vmem_limit_bytes=vmem)


def _log2(n):
    b = n.bit_length() - 1
    assert (1 << b) == n, n
    return b


def _adaln_kernel(c_ref, w_ref, b_ref, o_ref):
    c = c_ref[...]
    s = c * jax.nn.sigmoid(c)
    o_ref[...] = jnp.dot(s, w_ref[...], precision=HI, preferred_element_type=F32) + b_ref[...]


def _adaln(c8, w_ada, b_ada):
    D = c8.shape[1]
    n = w_ada.shape[1] // D
    return pl.pallas_call(
        _adaln_kernel,
        out_shape=jax.ShapeDtypeStruct((c8.shape[0], n * D), F32),
        grid=(n,),
        in_specs=[pl.BlockSpec((c8.shape[0], D), lambda j: (0, 0)),
                  pl.BlockSpec((D, D), lambda j: (0, j)),
                  pl.BlockSpec((1, D), lambda j: (0, j))],
        out_specs=pl.BlockSpec((c8.shape[0], D), lambda j: (0, j)),
        compiler_params=_params(1),
        name="adaln",
    )(c8, w_ada, b_ada.reshape(1, -1))


def _rms_mod(x, g, sc, sh):
    y = x * lax.rsqrt(jnp.mean(x * x, axis=-1, keepdims=True) + EPS)
    return (y * g) * (1.0 + sc) + sh


_CONTRACT_LAST = (((1,), (1,)), ((), ()))
_CONTRACT_FIRST = (((0,), (0,)), ((), ()))


def _inproj_kernel(x_ref, g_ref, sc_ref, sh_ref, *refs, n_tok, t_scales):
    n_fm = len(t_scales)
    w_refs, wt_refs = refs[:n_tok], refs[n_tok:n_tok + n_fm]
    wm_ref, wmt_ref = refs[n_tok + n_fm:n_tok + n_fm + 2]
    out_refs = refs[n_tok + n_fm + 2:]
    h = _rms_mod(x_ref[0], g_ref[...], sc_ref[0], sh_ref[0])
    hb = h.astype(BF16)
    ht = h.T
    htb = ht.astype(BF16)
    h_lo = (h - hb.astype(F32)).astype(BF16)
    ht_lo = (ht - htb.astype(F32)).astype(BF16)
    pt = jnp.dot(wmt_ref[...], htb, preferred_element_type=F32)
    out_refs[-1][0] = (pt[:MISC_W] + pt[MISC_W:]
                       + jnp.dot(wmt_ref[:MISC_W, :], ht_lo, preferred_element_type=F32))
    pm = jnp.dot(hb, wm_ref[...], preferred_element_type=F32)
    out_refs[-2][0] = (pm[:, :MISC_W] + pm[:, MISC_W:]
                       + jnp.dot(h_lo, wm_ref[:, :MISC_W], preferred_element_type=F32))
    for o_ref, w_ref in zip(out_refs[:n_tok], w_refs):
        o_ref[0] = jnp.dot(hb, w_ref[...], preferred_element_type=F32).astype(BF16)
    for o_ref, wt_ref, scale in zip(out_refs[n_tok:-2], wt_refs, t_scales):
        hgt = wt_ref.shape[0]
        val = jnp.dot(wt_ref[...], htb, preferred_element_type=F32)
        val = (val if scale == 1.0 else val * scale).astype(BF16)
        if len(o_ref.shape) == 4:
            o_ref[0, 0] = val
        elif o_ref.shape[1] == hgt:
            o_ref[0] = val
        else:
            for j in range(hgt // IDX_DIM):
                o_ref[0, j * LANES:j * LANES + IDX_DIM, :] = val[j * IDX_DIM:(j + 1) * IDX_DIM, :]
                o_ref[0, j * LANES + IDX_DIM:(j + 1) * LANES, :] = jnp.zeros(
                    (LANES - IDX_DIM, val.shape[1]), BF16)


def _inproj(x, g, sc, sh, w_tok, w_fm, w_misc, t_out_heights, t_scales, tm):
    B, T, D = x.shape
    row = lambda b, i: (b, i, 0)
    col = lambda b, i: (b, 0, i)
    per_b = lambda b, i: (b, 0, 0)
    const = lambda b, i: (0, 0)
    resident = functools.partial(pl.BlockSpec, index_map=const, pipeline_mode=pl.Buffered(1))
    weights = list(w_tok) + list(w_fm) + [w_misc, w_misc.T]
    out_shape = [jax.ShapeDtypeStruct((B, T, w.shape[1]), BF16) for w in w_tok]
    out_specs = [pl.BlockSpec((1, tm, w.shape[1]), row) for w in w_tok]
    out_shape += [jax.ShapeDtypeStruct((B, hgt, T), BF16) for hgt in t_out_heights[:-1]]
    out_specs += [pl.BlockSpec((1, hgt, tm), col) for hgt in t_out_heights[:-1]]
    out_shape += [jax.ShapeDtypeStruct((B, T // tm, t_out_heights[-1], tm), BF16),
                  jax.ShapeDtypeStruct((B, T, MISC_W), F32),
                  jax.ShapeDtypeStruct((B, MISC_W, T), F32)]
    out_specs += [pl.BlockSpec((1, 1, t_out_heights[-1], tm), lambda b, i: (b, i, 0, 0)),
                  pl.BlockSpec((1, tm, MISC_W), row),
                  pl.BlockSpec((1, MISC_W, tm), col)]
    return pl.pallas_call(
        functools.partial(_inproj_kernel, n_tok=len(w_tok), t_scales=t_scales),
        out_shape=out_shape,
        grid=(B, T // tm),
        in_specs=[pl.BlockSpec((1, tm, D), row),
                  pl.BlockSpec((1, D), const),
                  pl.BlockSpec((1, 1, D), per_b),
                  pl.BlockSpec((1, 1, D), per_b)] + [resident(w.shape) for w in weights],
        out_specs=out_specs,
        compiler_params=_params(2),
        name="inproj",
    )(x, g, sc, sh, *weights)


def _split_bf16(w):
    hi = w.astype(BF16)
    return jnp.concatenate([hi, (w - hi.astype(F32)).astype(BF16)], axis=-1)


def _dsa_kernel(q_ref, k_ref, vt_ref, qi_ref, mall_ref, mt_ref, o_ref, sc_ref, lg0_ref, lg1_ref,
                acc_ref, *, T, DQ, KT, KC, topk, dh):
    i = pl.program_id(1)
    lg_refs = (lg0_ref, lg1_ref)
    nt = ((i + 1) * DQ + KT - 1) // KT
    qpos = i * DQ + lax.broadcasted_iota(I32, (1, DQ), 1)
    lim = ((qpos >> _log2(CHUNK)) + 1) << _log2(CHUNK)
    krow = lax.broadcasted_iota(I32, (KT, DQ), 0)

    def tile(kt):
        return pl.ds(pl.multiple_of(kt * KT, KT), KT)

    def half_tile(kt):
        return pl.ds(pl.multiple_of(kt * (KT // 2), KT // 2), KT // 2)

    def trunc_bf16(x):
        bits = lax.bitcast_convert_type(x, I32) & jnp.int32(-65536)
        return lax.bitcast_convert_type(bits, F32).astype(BF16)

    wi_t = mt_ref[0, WI_OFF:WI_OFF + IDX_HEADS, :] * ((IDX_HEADS ** -0.5) * (IDX_DIM ** -0.5))

    def score_tile(kt, carry):
        ki_t = mall_ref[0, tile(kt), :].astype(BF16)
        s = jnp.zeros((KT, DQ), F32)
        for h in range(IDX_HEADS):
            x = jnp.dot(ki_t, qi_ref[0, h * LANES:(h + 1) * LANES, :], preferred_element_type=F32)
            s = s + jnp.maximum(x, 0.0) * wi_t[h:h + 1, :]
        s = jnp.where(krow + kt * KT < lim, s, NEG_SCORE)
        sc_ref[tile(kt), :] = s
        lg1_ref[half_tile(kt), :] = pltpu.bitcast(trunc_bf16(s), F32)
        return carry

    lax.fori_loop(0, nt, score_tile, 0)

    def count(pred):
        def body(kt, acc):
            m = jnp.where(pred(sc_ref[tile(kt), :], kt * KT), 1.0, 0.0)
            return acc + jnp.sum(m, axis=0, keepdims=True)
        return lax.fori_loop(0, nt, body, jnp.zeros((1, DQ), F32))

    int_min = jnp.int32(-2 ** 31)

    def key_to_float(u):
        bits = jnp.where(u < 0, u ^ int_min, ~u)
        return lax.bitcast_convert_type(bits, F32)

    one_b, zero_b = jnp.ones((), BF16), jnp.zeros((), BF16)

    def count_hi(tb):
        def body(kt, acc):
            hi = pltpu.bitcast(lg1_ref[half_tile(kt), :], BF16)
            m = jnp.where(hi >= tb, one_b, zero_b)
            for j in range(KT // BF16_ROWS):
                acc = acc + m[j * BF16_ROWS:(j + 1) * BF16_ROWS, :]
            return acc
        acc = lax.fori_loop(0, nt, body, jnp.zeros((BF16_ROWS, DQ), BF16))
        return jnp.sum(acc.astype(F32), axis=0, keepdims=True)

    def hi_body(it, carry):
        u, c_ge = carry
        trial = u | (jnp.int32(1) << (KEY_BITS - 1 - it))
        cnt = count_hi(trunc_bf16(key_to_float(trial)))
        ok = cnt >= topk
        return jnp.where(ok, trial, u), jnp.where(ok, cnt, c_ge)

    def lo_body(it, carry):
        u, c_ge = carry
        trial = u | (jnp.int32(1) << (KEY_BITS // 2 - 1 - it))
        t = key_to_float(trial)
        cnt = count(lambda s, c0: s >= t)
        ok = cnt >= topk
        return jnp.where(ok, trial, u), jnp.where(ok, cnt, c_ge)

    carry = (jnp.zeros((1, DQ), I32), jnp.zeros((1, DQ), F32))
    carry = lax.fori_loop(0, KEY_BITS // 2, hi_body, carry)
    u, c_ge = lax.fori_loop(0, KEY_BITS // 2, lo_body, carry)
    thr = key_to_float(u)

    c_gt = count(lambda s, c0: s > thr)
    need = topk - c_gt
    excess = jnp.logical_and(c_ge - c_gt > need, thr > NEG_SCORE)
    has_excess = jnp.max(jnp.where(excess, 1.0, 0.0)) > 0.0
    few = thr <= NEG_SCORE
    need_ties = jnp.where(few, 0.0, need)
    thr_ge = jnp.where(few, NEG_SCORE_NEXT, thr)

    def bias_no_ties():
        def body(kt, carry):
            sc_ref[tile(kt), :] = jnp.where(sc_ref[tile(kt), :] >= thr_ge, 0.0, NEG_BIAS)
            return carry
        lax.fori_loop(0, nt, body, 0)

    def bias_with_ties():
        r = lax.broadcasted_iota(I32, (KT, KT), 0)
        c = lax.broadcasted_iota(I32, (KT, KT), 1)
        before = jnp.where(c < r, 1.0, 0.0).astype(BF16)

        def body(kt, seen):
            s = sc_ref[tile(kt), :]
            eq = jnp.where(s == thr, 1.0, 0.0)
            rank = seen + jnp.dot(before, eq.astype(BF16), preferred_element_type=F32)
            keep = jnp.where(s > thr, 1.0, jnp.where(rank < need_ties, eq, 0.0))
            sc_ref[tile(kt), :] = jnp.where(keep > 0.5, 0.0, NEG_BIAS)
            return seen + jnp.sum(eq, axis=0, keepdims=True)
        lax.fori_loop(0, nt, body, jnp.zeros((1, DQ), F32))

    lax.cond(has_excess, bias_with_ties, bias_no_ties)

    def fused(h_b, h_a, kts, mx_b, den, mx_a):
        if h_a is not None:
            hs_a = slice(h_a * dh, (h_a + 1) * dh)
            lgs = [jnp.dot(k_ref[0, tile(kt), hs_a], q_ref[0, hs_a, :], preferred_element_type=F32)
                   for kt in kts]
        if h_b is not None:
            hs_b = slice(h_b * dh, (h_b + 1) * dh)
            ps = [jnp.exp2(lg_refs[h_b % 2][tile(kt), :] - mx_b) for kt in kts]
            pv = jnp.zeros((dh, DQ), F32)
            for kt, p in zip(kts, ps):
                pb = p.astype(BF16)
                for j in range(KT // KC):
                    pv = pv + jnp.dot(vt_ref[0, kt * (KT // KC) + j, hs_b, :],
                                      pb[j * KC:(j + 1) * KC, :], preferred_element_type=F32)
                den = den + jnp.sum(p, axis=0, keepdims=True)
            acc_ref[hs_b, :] += pv
        if h_a is not None:
            for kt, lg in zip(kts, lgs):
                s = lg + sc_ref[tile(kt), :]
                lg_refs[h_a % 2][tile(kt), :] = s
                mx_a = jnp.maximum(mx_a, jnp.max(s, axis=0, keepdims=True))
        return den, mx_a

    def tile_loop(h_b, h_a, mx_b):
        def run(first, count, carry):
            for i in range(count):
                carry = fused(h_b, h_a, [first + i], mx_b, *carry)
            return carry

        carry = lax.fori_loop(0, nt // ATT_UNROLL,
                              lambda j, c: run(ATT_UNROLL * j, ATT_UNROLL, c), (den0, mx0))
        done = (nt // ATT_UNROLL) * ATT_UNROLL
        width = ATT_UNROLL // 2
        while width >= 1:
            take = (nt & width) != 0
            carry = lax.cond(take, functools.partial(run, done, width), lambda c: c, carry)
            done = done + jnp.where(take, width, 0)
            width //= 2
        return carry

    mx0 = jnp.full((1, DQ), NEG_BIAS, F32)
    den0 = jnp.zeros((1, DQ), F32)
    acc_ref[...] = jnp.zeros(acc_ref.shape, F32)
    _, mx = tile_loop(None, 0, None)
    for h in range(DSA_HEADS):
        den, mx_next = tile_loop(h, h + 1 if h + 1 < DSA_HEADS else None, mx)
        hs = slice(h * dh, (h + 1) * dh)
        o_ref[0, :, hs] = (acc_ref[hs, :] / den).T.astype(BF16)
        mx = mx_next


def _dsa(qa_t, ka, va_t, qi_t, misc, misc_t, dq):
    B, T, D = ka.shape
    dh = D // DSA_HEADS
    KC = va_t.shape[3]
    KT = min(KEY_TILE, T)
    topk = min(IDX_TOPK_MAX, T // 4)
    assert KT >= topk and KT % KC == 0 and dq % CHUNK == 0
    assert T // BF16_ROWS <= 256, "packed bf16 counters must stay exact"
    blk = lambda b, i: (b, i, 0)
    whole = functools.partial(pl.BlockSpec, pipeline_mode=pl.Buffered(1))
    return pl.pallas_call(
        functools.partial(_dsa_kernel, T=T, DQ=dq, KT=KT, KC=KC, topk=topk, dh=dh),
        out_shape=jax.ShapeDtypeStruct((B, T, D), BF16),
        grid=(B, T // dq),
        in_specs=[pl.BlockSpec((1, D, dq), lambda b, i: (b, 0, i)),
                  whole((1, T, D), lambda b, i: (b, 0, 0)),
                  whole((1, T // KC, D, KC), lambda b, i: (b, 0, 0, 0)),
                  pl.BlockSpec((1, IDX_HEADS * LANES, dq), lambda b, i: (b, 0, i)),
                  whole((1, T, MISC_W), lambda b, i: (b, 0, 0)),
                  pl.BlockSpec((1, MISC_W, dq), lambda b, i: (b, 0, i))],
        out_specs=pl.BlockSpec((1, dq, D), blk),
        scratch_shapes=[pltpu.VMEM((T, dq), F32),
                        pltpu.VMEM((T, dq), F32),
                        pltpu.VMEM((T, dq), F32),
                        pltpu.VMEM((D, dq), F32)],
        compiler_params=_params(2),
        name="dsa",
    )(qa_t, ka, va_t, qi_t, misc, misc_t)


def _gla_kernel(q_ref, k_ref, v_ref, gb_ref, misc_ref, wg_ref, bg_ref, gn_ref, o_ref, st_ref,
                *, tb, dk, dv):
    @pl.when(pl.program_id(1) == 0)
    def _():
        st_ref[...] = jnp.zeros(st_ref.shape, F32)

    hk = wg_ref.shape[1] // 2
    nbb = q_ref.shape[0]
    las = []
    for bb in range(nbb):
        m = misc_ref[bb]
        m_hi = m.astype(BF16)
        m_lo = (m - m_hi.astype(F32)).astype(BF16)
        px = jnp.dot(m_hi, wg_ref[...], preferred_element_type=F32)
        x = (px[:, :hk] + px[:, hk:]
             + jnp.dot(m_lo, wg_ref[:, :hk], preferred_element_type=F32)) + bg_ref[...]
        log_a = (jnp.minimum(x, 0.0) - jnp.log1p(jnp.exp(-jnp.abs(x)))) / GATE_TAU
        la1 = log_a.astype(BF16)
        res = log_a - la1.astype(F32)
        la2 = res.astype(BF16)
        las.append((la1, la2, (res - la2.astype(F32)).astype(BF16)))

    r = lax.broadcasted_iota(I32, (CHUNK, CHUNK), 0)
    c = lax.broadcasted_iota(I32, (CHUNK, CHUNK), 1)
    causal = r >= c
    tril = jnp.where(causal, 1.0, 0.0).astype(BF16)
    gn = gn_ref[...]
    contract_last = (((1,), (1,)), ((), ()))
    contract_first = (((0,), (0,)), ((), ()))

    units = [(bb, h) for bb in range(nbb) for h in range(GLA_HEADS)]
    for ci in range(tb // CHUNK):
        rs = slice(ci * CHUNK, (ci + 1) * CHUNK)
        g_all = [sum(jnp.dot(tril, la[rs], preferred_element_type=F32) for la in las[bb])
                 for bb in range(nbb)]
        qe, kd, kdl, eg_last = {}, {}, {}, {}
        for u in units:
            bb, h = u
            ks = slice(h * dk, (h + 1) * dk)
            g = g_all[bb][:, ks]
            g_last = g[CHUNK - 1:CHUNK, :]
            kh = k_ref[bb, rs, ks].astype(F32)
            qe[u] = ((q_ref[bb, rs, ks].astype(F32) * (dk ** -0.5)) * jnp.exp(g)).astype(BF16)
            kd[u] = (kh * jnp.exp(-g)).astype(BF16)
            kdl[u] = (kh * jnp.exp(g_last - g)).astype(BF16)
            eg_last[u] = jnp.exp(g_last)
        att = {}
        for u in units:
            a = lax.dot_general(qe[u], kd[u], contract_last, preferred_element_type=F32)
            att[u] = jnp.where(causal, a, 0.0).astype(BF16)
        outs = {}
        for u in units:
            bb, h = u
            vh = v_ref[bb, rs, h * dv:(h + 1) * dv]
            st = st_ref[bb, h]
            outs[u] = (jnp.dot(att[u], vh, preferred_element_type=F32)
                       + lax.dot_general(qe[u], st.astype(BF16), contract_last,
                                         preferred_element_type=F32))
            upd = lax.dot_general(vh, kdl[u], contract_first, preferred_element_type=F32)
            st_ref[bb, h] = st * eg_last[u] + upd
        for u in units:
            bb, h = u
            vs = slice(h * dv, (h + 1) * dv)
            o = outs[u]
            y = o * lax.rsqrt(jnp.mean(o * o, axis=-1, keepdims=True) + EPS) * gn
            gate = gb_ref[bb, rs, vs].astype(F32)
            o_ref[bb, rs, vs] = (y * (gate * jax.nn.sigmoid(gate))).astype(BF16)


def _gla(qb, kb, vb, gb, misc, wg_pad, b_gate, gla_norm_g, tb):
    B, T, HK = qb.shape
    HV = vb.shape[2]
    dk, dv = HK // GLA_HEADS, HV // GLA_HEADS
    nbb = 2 if B % 2 == 0 else 1
    blk = lambda b, i: (b, i, 0)
    const = lambda b, i: (0, 0)
    return pl.pallas_call(
        functools.partial(_gla_kernel, tb=tb, dk=dk, dv=dv),
        out_shape=jax.ShapeDtypeStruct((B, T, HV), BF16),
        grid=(B // nbb, T // tb),
        in_specs=[pl.BlockSpec((nbb, tb, HK), blk),
                  pl.BlockSpec((nbb, tb, HK), blk),
                  pl.BlockSpec((nbb, tb, HV), blk),
                  pl.BlockSpec((nbb, tb, HV), blk),
                  pl.BlockSpec((nbb, tb, MISC_W), blk),
                  pl.BlockSpec((MISC_W, 2 * HK), const),
                  pl.BlockSpec((1, HK), const),
                  pl.BlockSpec((1, dv), const)],
        out_specs=pl.BlockSpec((nbb, tb, HV), blk),
        scratch_shapes=[pltpu.VMEM((nbb, GLA_HEADS, dv, dk), F32)],
        compiler_params=_params(2),
        name="gla",
    )(qb, kb, vb, gb, misc, wg_pad, b_gate.reshape(1, -1), gla_norm_g.reshape(1, -1))


def _pack_bf16_pairs(x):
    half = x.shape[1] // 2
    bits = lax.bitcast_convert_type(x.astype(BF16).astype(F32), I32)
    return (bits[:, half:] & jnp.int32(-65536)) | lax.shift_right_logical(bits[:, :half], 16)


def _unpack_bf16_pairs(w):
    lo = lax.bitcast_convert_type(w << 16, F32)
    hi = lax.bitcast_convert_type(w & jnp.int32(-65536), F32)
    return jnp.concatenate([lo, hi], axis=1)


def _merge_kernel(ya_ref, yb_ref, ga_ref, gbm_ref, x_ref, gt_ref, sc_ref, sh_ref, g2_ref,
                  wa_ref, wb_ref, wo_ref, rw_ref, rb_ref,
                  x1_ref, h2_ref, e_ref, w_ref, cnt_ref, *, n_exp):
    pa = jnp.dot(ya_ref[0], wa_ref[...], preferred_element_type=F32)
    pb = jnp.dot(yb_ref[0], wb_ref[...], preferred_element_type=F32)
    merged = (jax.nn.sigmoid(ga_ref[0].astype(F32)) * pa
              + jax.nn.sigmoid(gbm_ref[0].astype(F32)) * pb)
    mo = jnp.dot(merged.astype(BF16), wo_ref[...], preferred_element_type=F32)
    x1 = x_ref[0] + gt_ref[0] * mo
    x1_ref[0] = x1
    h2 = _rms_mod(x1, g2_ref[...], sc_ref[0], sh_ref[0])
    h2_ref[0] = _pack_bf16_pairs(h2)

    h_hi = h2.astype(BF16)
    h_lo = (h2 - h_hi.astype(F32)).astype(BF16)
    part = jnp.dot(h_hi, rw_ref[...], preferred_element_type=F32)
    logits = (part[:, :LANES] + part[:, LANES:]
              + jnp.dot(h_lo, rw_ref[:, :LANES], preferred_element_type=F32)) + rb_ref[...]
    tm = logits.shape[0]
    lane = lax.broadcasted_iota(I32, (tm, LANES), 1)
    logits = jnp.where(lane < n_exp, logits, -jnp.inf)
    top_v, top_e = [], []
    for _ in range(TOP_K):
        mx = jnp.max(logits, axis=1, keepdims=True)
        idx = jnp.min(jnp.where(logits == mx, lane, LANES), axis=1, keepdims=True)
        logits = jnp.where(lane == idx, -jnp.inf, logits)
        top_v.append(mx)
        top_e.append(idx)
    ex = [jnp.exp(v - top_v[0]) for v in top_v]
    den = ex[0]
    for t in ex[1:]:
        den = den + t
    e_out = jnp.zeros((tm, LANES), I32)
    w_out = jnp.zeros((tm, LANES), F32)
    hot = jnp.zeros((tm, LANES), F32)
    for k in range(TOP_K):
        e_out = jnp.where(lane == k, top_e[k], e_out)
        w_out = jnp.where(lane == k, ex[k] / den, w_out)
        hot = hot + jnp.where(lane == top_e[k], 1.0, 0.0)
    e_ref[0] = e_out
    w_ref[0] = w_out

    @pl.when(jnp.logical_and(pl.program_id(0) == 0, pl.program_id(1) == 0))
    def _():
        cnt_ref[...] = jnp.zeros(cnt_ref.shape, F32)

    cnt_ref[0:1, :] += jnp.sum(hot, axis=0, keepdims=True)


def _merge(ya, yb, ga, gbm, x, gt1, sc2, sh2, g2, wa, wb, wo, rw_pad, rb_pad, n_exp, tm):
    B, T, D = x.shape
    row = lambda b, i: (b, i, 0)
    per_b = lambda b, i: (b, 0, 0)
    const = lambda b, i: (0, 0)
    resident = functools.partial(pl.BlockSpec, index_map=const, pipeline_mode=pl.Buffered(1))
    return pl.pallas_call(
        functools.partial(_merge_kernel, n_exp=n_exp),
        out_shape=[jax.ShapeDtypeStruct((B, T, D), F32),
                   jax.ShapeDtypeStruct((B, T, D // 2), I32),
                   jax.ShapeDtypeStruct((B, T, LANES), I32),
                   jax.ShapeDtypeStruct((B, T, LANES), F32),
                   jax.ShapeDtypeStruct((8, LANES), F32)],
        grid=(B, T // tm),
        in_specs=[pl.BlockSpec((1, tm, D), row),
                  pl.BlockSpec((1, tm, D), row),
                  pl.BlockSpec((1, tm, D), row),
                  pl.BlockSpec((1, tm, D), row),
                  pl.BlockSpec((1, tm, D), row),
                  pl.BlockSpec((1, 1, D), per_b),
                  pl.BlockSpec((1, 1, D), per_b),
                  pl.BlockSpec((1, 1, D), per_b),
                  pl.BlockSpec((1, D), const),
                  resident((D, D)), resident((D, D)), resident((D, D)),
                  resident((D, 2 * LANES)), pl.BlockSpec((1, LANES), const)],
        out_specs=[pl.BlockSpec((1, tm, D), row),
                   pl.BlockSpec((1, tm, D // 2), row),
                   pl.BlockSpec((1, tm, LANES), row),
                   pl.BlockSpec((1, tm, LANES), row),
                   pl.BlockSpec((8, LANES), const)],
        compiler_params=_params(2),
        name="merge",
    )(ya, yb, ga, gbm, x, gt1, sc2, sh2, g2, wa, wb, wo, rw_pad, rb_pad)


def _plan_kernel(e_ref, cnt_ref, slot_ref, blk_ref, nblk_ref, base_ref, *, n_exp, tmb, nbp):
    tt = e_ref.shape[0]
    lane1 = lax.broadcasted_iota(I32, (1, LANES), 1)

    @pl.when(pl.program_id(0) == 0)
    def _():
        cnt = cnt_ref[0:1, :]
        padded = jnp.where(lane1 < n_exp, jnp.ceil(cnt / tmb) * tmb, 0.0)
        r = lax.broadcasted_iota(I32, (LANES, LANES), 0)
        c = lax.broadcasted_iota(I32, (LANES, LANES), 1)
        upper = jnp.where(r < c, 1.0, 0.0)
        start = jnp.dot(jnp.broadcast_to(padded, (8, LANES)), upper, precision=HI,
                        preferred_element_type=F32)[0:1, :]
        base_ref[...] = start
        end = start + padded
        jrow = lax.broadcasted_iota(I32, (nbp, LANES), 0).astype(F32) * tmb
        owner = jnp.sum(jnp.where(jnp.logical_and(end <= jrow, lane1 < n_exp), 1.0, 0.0),
                        axis=1, keepdims=True)
        total = jnp.sum(padded, axis=1, keepdims=True)
        last = jnp.sum(jnp.where(jnp.logical_and(end == jrow + tmb, padded > 0.0), 1.0, 0.0),
                       axis=1, keepdims=True)
        sparse_blk = jnp.where(jrow >= total, 1.0, last)
        lane_n = lax.broadcasted_iota(I32, (nbp, LANES), 1)
        blk_ref[...] = jnp.where(lane_n == 1, sparse_blk,
                                 jnp.minimum(owner, n_exp - 1.0)).astype(I32)
        nblk_ref[...] = jnp.broadcast_to(total / tmb, (8, LANES)).astype(I32)

    e = e_ref[...]
    lane = lax.broadcasted_iota(I32, (tt, LANES), 1)
    hot = jnp.zeros((tt, LANES), F32)
    for k in range(TOP_K):
        hot = hot + jnp.where(lane == e[:, k:k + 1], 1.0, 0.0)
    r = lax.broadcasted_iota(I32, (tt, tt), 0)
    c = lax.broadcasted_iota(I32, (tt, tt), 1)
    lower = jnp.where(r > c, 1.0, 0.0).astype(BF16)
    rank = jnp.dot(lower, hot.astype(BF16), preferred_element_type=F32)
    pos = base_ref[...] + rank
    out = jnp.zeros((tt, LANES), F32)
    for k in range(TOP_K):
        sk = jnp.sum(jnp.where(lane == e[:, k:k + 1], pos, 0.0), axis=1, keepdims=True)
        out = jnp.where(lane == k, sk, out)
    slot_ref[...] = out.astype(I32)
    base_ref[...] += jnp.sum(hot, axis=0, keepdims=True)


def _plan(e_pad, cnt, n_exp, tmb, nbp, tt):
    N = e_pad.shape[0]
    return pl.pallas_call(
        functools.partial(_plan_kernel, n_exp=n_exp, tmb=tmb, nbp=nbp),
        out_shape=[jax.ShapeDtypeStruct((N, LANES), I32),
                   jax.ShapeDtypeStruct((nbp, LANES), I32),
                   jax.ShapeDtypeStruct((8, LANES), I32)],
        grid=(N // tt,),
        in_specs=[pl.BlockSpec((tt, LANES), lambda i: (i, 0)),
                  pl.BlockSpec((8, LANES), lambda i: (0, 0))],
        out_specs=[pl.BlockSpec((tt, LANES), lambda i: (i, 0)),
                   pl.BlockSpec((nbp, LANES), lambda i: (0, 0)),
                   pl.BlockSpec((8, LANES), lambda i: (0, 0))],
        scratch_shapes=[pltpu.VMEM((1, LANES), F32)],
        compiler_params=_params(1),
        name="moe_plan",
    )(e_pad, cnt)


def _dispatch_kernel(slot_ref, sparse_ref, h_ref, xs_ref, zeros, sem, zsem, *, tt, tmb):
    @pl.when(pl.program_id(0) == 0)
    def _():
        zeros[...] = jnp.zeros(zeros.shape, zeros.dtype)

        def clear(j):
            return pltpu.make_async_copy(
                zeros, xs_ref.at[pl.ds(pl.multiple_of(j * tmb, tmb), tmb)], zsem)

        def start_clear(j, carry):
            @pl.when(sparse_ref[j] != 0)
            def _():
                clear(j).start()
            return carry

        def wait_clear(j, carry):
            @pl.when(sparse_ref[j] != 0)
            def _():
                clear(j).wait()
            return carry

        lax.fori_loop(0, sparse_ref.shape[0], start_clear, 0)
        lax.fori_loop(0, sparse_ref.shape[0], wait_clear, 0)

    def start(g, carry):
        t0 = pl.multiple_of(g * ROW_GROUP, ROW_GROUP)
        for j in range(ROW_GROUP):
            for k in range(TOP_K):
                s = slot_ref[(t0 + j) * TOP_K + k]
                pltpu.make_async_copy(h_ref.at[pl.ds(t0 + j, 1)], xs_ref.at[pl.ds(s, 1)],
                                      sem).start(priority=k % 2)
        return carry

    lax.fori_loop(0, tt // ROW_GROUP, start, 0)
    for k in range(TOP_K):
        pltpu.make_async_copy(h_ref, xs_ref.at[pl.ds(0, tt)], sem).wait()


def _dispatch(slots_flat, sparse_blk, h2, tt, tmb):
    N, D = h2.shape
    nb = sparse_blk.shape[0]
    return pl.pallas_call(
        functools.partial(_dispatch_kernel, tt=tt, tmb=tmb),
        out_shape=jax.ShapeDtypeStruct((nb * tmb, D), h2.dtype),
        grid=(N // tt,),
        in_specs=[pl.BlockSpec((tt * TOP_K,), lambda i: (i,), memory_space=pltpu.SMEM),
                  pl.BlockSpec((nb,), lambda i: (0,), memory_space=pltpu.SMEM),
                  pl.BlockSpec((tt, D), lambda i: (i, 0))],
        out_specs=pl.BlockSpec(memory_space=pl.ANY),
        scratch_shapes=[pltpu.VMEM((tmb, D), h2.dtype), pltpu.SemaphoreType.DMA,
                        pltpu.SemaphoreType.DMA],
        compiler_params=_params(1),
        name="moe_dispatch",
    )(slots_flat, sparse_blk, h2)


def _experts_kernel(blk_ref, nblk_ref, xs_ref, wgu_ref, bgu_ref, wd_ref, bd_ref, ys_ref,
                    wgu_bf, wd_bf, *, ff):
    j = pl.program_id(0)
    e = blk_ref[j]
    fresh = jnp.logical_or(j == 0, e != blk_ref[jnp.maximum(j - 1, 0)])
    active = j < nblk_ref[0]

    @pl.when(jnp.logical_and(active, fresh))
    def _():
        wgu_bf[...] = wgu_ref[0].astype(BF16)
        wd_bf[...] = wd_ref[0].astype(BF16)

    @pl.when(active)
    def _():
        x = _unpack_bf16_pairs(xs_ref[...]).astype(BF16)
        hgu = jnp.dot(x, wgu_bf[...], preferred_element_type=F32) + bgu_ref[0]
        gate = jnp.minimum(hgu[:, :ff], SWIGLU_LIMIT)
        up = jnp.clip(hgu[:, ff:], -SWIGLU_LIMIT, SWIGLU_LIMIT)
        act = (up + 1.0) * (gate * jax.nn.sigmoid(SWIGLU_ALPHA * gate))
        y = jnp.dot(act.astype(BF16), wd_bf[...], preferred_element_type=F32) + bd_ref[0]
        ys_ref[...] = _pack_bf16_pairs(y)

    @pl.when(jnp.logical_not(active))
    def _():
        ys_ref[...] = jnp.zeros(ys_ref.shape, I32)


def _experts(blk_e, nblk, xs, w_gate_up, b_gate_up, w_down, b_down, tmb):
    P, dp = xs.shape
    E, D, ff2 = w_gate_up.shape
    ff = ff2 // 2
    by_expert = lambda j, blk, nb: (blk[j], 0, 0)
    return pl.pallas_call(
        functools.partial(_experts_kernel, ff=ff),
        out_shape=jax.ShapeDtypeStruct((P, dp), I32),
        grid_spec=pltpu.PrefetchScalarGridSpec(
            num_scalar_prefetch=2,
            grid=(P // tmb,),
            in_specs=[pl.BlockSpec((tmb, dp), lambda j, blk, nb: (j, 0)),
                      pl.BlockSpec((1, D, ff2), by_expert),
                      pl.BlockSpec((1, 1, ff2), by_expert),
                      pl.BlockSpec((1, ff, D), by_expert),
                      pl.BlockSpec((1, 1, D), by_expert)],
            out_specs=pl.BlockSpec((tmb, dp), lambda j, blk, nb: (j, 0)),
            scratch_shapes=[pltpu.VMEM((D, ff2), BF16), pltpu.VMEM((ff, D), BF16)]),
        compiler_params=_params(1),
        name="moe_experts",
    )(blk_e, nblk, xs, w_gate_up, b_gate_up.reshape(E, 1, ff2), w_down, b_down.reshape(E, 1, D))


def _combine_kernel(slot_ref, ys_ref, w_ref, x1_ref, gt_ref, fg_ref, o_ref, buf, sem, *, tt, nt):
    step = pl.program_id(0) * nt + pl.program_id(1)
    nsteps = pl.num_programs(0) * nt

    def issue(s):
        b = s & 1

        def body(g, carry):
            t0 = pl.multiple_of(g * ROW_GROUP, ROW_GROUP)
            for j in range(ROW_GROUP):
                for k in range(TOP_K):
                    idx = slot_ref[(s * tt + t0 + j) * TOP_K + k]
                    pltpu.make_async_copy(ys_ref.at[pl.ds(idx, 1)], buf.at[b, k, pl.ds(t0 + j, 1)],
                                          sem.at[b]).start(priority=k % 2)
            return carry
        lax.fori_loop(0, tt // ROW_GROUP, body, 0)

    @pl.when(step == 0)
    def _():
        issue(step)

    @pl.when(step + 1 < nsteps)
    def _():
        issue(step + 1)

    cur = step & 1
    for k in range(TOP_K):
        pltpu.make_async_copy(ys_ref.at[pl.ds(0, tt)], buf.at[cur, k], sem.at[cur]).wait()

    w = w_ref[0]
    moe = _unpack_bf16_pairs(buf[cur, 0]) * w[:, 0:1]
    for k in range(1, TOP_K):
        moe = moe + _unpack_bf16_pairs(buf[cur, k]) * w[:, k:k + 1]
    x2 = x1_ref[0] + gt_ref[0] * moe
    y = x2 * lax.rsqrt(jnp.mean(x2 * x2, axis=-1, keepdims=True) + EPS)
    o_ref[0] = y * fg_ref[...]


def _combine(slots_flat, ys, w_pad, x1, gt2, final_g, tt):
    B, T, D = x1.shape
    nt = T // tt
    row = lambda b, i, slots: (b, i, 0)
    return pl.pallas_call(
        functools.partial(_combine_kernel, tt=tt, nt=nt),
        out_shape=jax.ShapeDtypeStruct((B, T, D), F32),
        grid_spec=pltpu.PrefetchScalarGridSpec(
            num_scalar_prefetch=1,
            grid=(B, nt),
            in_specs=[pl.BlockSpec(memory_space=pl.ANY),
                      pl.BlockSpec((1, tt, LANES), row),
                      pl.BlockSpec((1, tt, D), row),
                      pl.BlockSpec((1, 1, D), lambda b, i, slots: (b, 0, 0)),
                      pl.BlockSpec((1, D), lambda b, i, slots: (0, 0))],
            out_specs=pl.BlockSpec((1, tt, D), row),
            scratch_shapes=[pltpu.VMEM((2, TOP_K, tt, ys.shape[1]), ys.dtype),
                            pltpu.SemaphoreType.DMA((2,))]),
        compiler_params=_params(2),
        name="moe_combine",
    )(slots_flat, ys, w_pad, x1, gt2, final_g.reshape(1, -1))


class _Tiles(NamedTuple):
    inproj: int
    dsa_q: int
    gla: int
    merge: int
    plan: int
    dispatch: int
    expert: int
    combine: int


def _tiles(T):
    return _Tiles(inproj=min(512, T), dsa_q=min(512, T), gla=min(512, T), merge=min(1024, T),
                  plan=min(1024, T), dispatch=min(256, T), expert=256, combine=min(256, T))


def _layer(x, c, w_ada, b_ada, norm1_g, w_in, w_gate_lr, b_gate, gla_norm_g, w_branch_a,
           w_branch_b, w_out, norm2_g, router_w, router_b, w_gate_up, b_gate_up, w_down, b_down,
           final_g):
    B, T, D = x.shape
    N = B * T
    tiles = _tiles(T)
    n_exp = router_w.shape[1]
    hk = w_gate_lr.shape[1]
    hv = w_branch_b.shape[0]
    ni = IDX_HEADS * IDX_DIM

    c8 = jnp.zeros((8, D), F32).at[:B].set(c)
    mod = _adaln(c8, w_ada, b_ada)[:B]
    sh1, sc1, gt1, sh2, sc2, gt2 = [m.reshape(B, 1, D) for m in jnp.split(mod, 6, axis=-1)]

    sizes = (D, D, D, ni, IDX_DIM, IDX_HEADS, hk, hk, hv, hv, GATE_RANK, D, D)
    offs = [0]
    for s in sizes:
        offs.append(offs[-1] + s)
    seg = lambda j: w_in[:, offs[j]:offs[j + 1]]
    w_tok = [seg(j).astype(BF16) for j in (1, 6, 7, 8, 9, 11, 12)]
    w_fm = [seg(j).T.astype(BF16) for j in (0, 3, 2)]
    t_out_heights = (sizes[0], IDX_HEADS * LANES, sizes[2])
    t_scales = ((D // DSA_HEADS) ** -0.5 * LOG2_E, 1.0, 1.0)
    w_misc = jnp.concatenate(
        [seg(4), seg(5), seg(10), jnp.zeros((D, MISC_W - IDX_DIM - IDX_HEADS - GATE_RANK), F32)],
        axis=1)
    ka, qb, kb, vb, gb, ga, gbm, qa_t, qi_t, va_t, misc, misc_t = _inproj(
        x, norm1_g.reshape(1, D), sc1, sh1, w_tok, w_fm, _split_bf16(w_misc), t_out_heights,
        t_scales, tiles.inproj)

    ya = _dsa(qa_t, ka, va_t, qi_t, misc, misc_t, tiles.dsa_q)

    wg_pad = jnp.zeros((MISC_W, hk), F32).at[LR_OFF:LR_OFF + GATE_RANK].set(w_gate_lr)
    yb = _gla(qb, kb, vb, gb, misc, _split_bf16(wg_pad), b_gate, gla_norm_g, tiles.gla)

    rw_pad = _split_bf16(jnp.zeros((D, LANES), F32).at[:, :n_exp].set(router_w))
    rb_pad = jnp.zeros((1, LANES), F32).at[0, :n_exp].set(router_b)
    x1, h2, e_pad, w_pad, cnt = _merge(
        ya, yb, ga, gbm, x, gt1, sc2, sh2, norm2_g.reshape(1, D),
        w_branch_a.astype(BF16), w_branch_b.astype(BF16), w_out.astype(BF16),
        rw_pad, rb_pad, n_exp, tiles.merge)

    tmb = tiles.expert
    nb = -(-N * TOP_K // tmb) + n_exp
    nbp = -(-nb // ROW_GROUP) * ROW_GROUP
    slots, blk_e, nblk = _plan(e_pad.reshape(N, LANES), cnt, n_exp, tmb, nbp, tiles.plan)

    slots_flat = slots[:, :TOP_K].reshape(-1)
    xs = _dispatch(slots_flat, blk_e[:nb, 1], h2.reshape(N, D // 2), tiles.dispatch, tmb)
    ys = _experts(blk_e[:nb, 0], nblk[0, :1], xs, w_gate_up, b_gate_up, w_down, b_down, tmb)
    return _combine(slots_flat, ys, w_pad, x1, gt2, final_g, tiles.combine)


def kernel(x, c, w_ada, b_ada, norm1_g, w_in, w_gate_lr, b_gate, gla_norm_g, w_branch_a, w_branch_b,
           w_out, norm2_g, router_w, router_b, w_gate_up, b_gate_up, w_down, b_down, final_g):
    assert w_ada.shape[0] == 1, "single-layer block"
    return _layer(x, c, w_ada[0], b_ada[0], norm1_g[0], w_in[0], w_gate_lr[0], b_gate[0],
                  gla_norm_g[0], w_branch_a[0], w_branch_b[0], w_out[0], norm2_g[0], router_w[0],
                  router_b[0], w_gate_up[0], b_gate_up[0], w_down[0], b_down[0], final_g)
```

```python
import functools
from typing import NamedTuple

import numpy as np
import jax
import jax.numpy as jnp
from jax import lax
from jax.experimental import pallas as pl
from jax.experimental.pallas import tpu as pltpu

CHUNK = 64
DSA_HEADS = 8
IDX_HEADS = 8
IDX_DIM = 64
IDX_TOPK_MAX = 256
GLA_HEADS = 4
GATE_RANK = 16
GATE_TAU = 16.0
TOP_K = 4
SWIGLU_LIMIT = 7.0
SWIGLU_ALPHA = 1.702
EPS = 1e-6

LANES = 128
ROW_GROUP = 8
BF16_ROWS = 16
ATT_UNROLL = 4
KEY_TILE = 512
KEY_BITS = 32
LOG2_E = 1.4426950408889634
MISC_W = LANES
WI_OFF, LR_OFF = IDX_DIM, IDX_DIM + IDX_HEADS
NEG_SCORE = -3.0e38
NEG_SCORE_NEXT = float(np.nextafter(np.float32(NEG_SCORE), np.float32(0.0)))
NEG_BIAS = -1.0e30
VMEM_LIMIT = 56 * 1024 * 1024

F32 = jnp.float32
BF16 = jnp.bfloat16
I32 = jnp.int32
HI = lax.Precision.HIGHEST


def _params(n_axes, vmem=VMEM_LIMIT):
    return pltpu.CompilerParams(dimension_semantics=("arbitrary",) * n_axes,
                                vmem_limit_bytes=vmem)


def _log2(n):
    b = n.bit_length() - 1
    assert (1 << b) == n, n
    return b


def _adaln_kernel(c_ref, w_ref, b_ref, o_ref):
    c = c_ref[...]
    s = c * jax.nn.sigmoid(c)
    o_ref[...] = jnp.dot(s, w_ref[...], precision=HI, preferred_element_type=F32) + b_ref[...]


def _adaln(c8, w_ada, b_ada):
    D = c8.shape[1]
    n = w_ada.shape[1] // D
    return pl.pallas_call(
        _adaln_kernel,
        out_shape=jax.ShapeDtypeStruct((c8.shape[0], n * D), F32),
        grid=(n,),
        in_specs=[pl.BlockSpec((c8.shape[0], D), lambda j: (0, 0)),
                  pl.BlockSpec((D, D), lambda j: (0, j)),
                  pl.BlockSpec((1, D), lambda j: (0, j))],
        out_specs=pl.BlockSpec((c8.shape[0], D), lambda j: (0, j)),
        compiler_params=_params(1),
        name="adaln",
    )(c8, w_ada, b_ada.reshape(1, -1))


def _rms_mod(x, g, sc, sh):
    y = x * lax.rsqrt(jnp.mean(x * x, axis=-1, keepdims=True) + EPS)
    return (y * g) * (1.0 + sc) + sh


_CONTRACT_LAST = (((1,), (1,)), ((), ()))
_CONTRACT_FIRST = (((0,), (0,)), ((), ()))


def _inproj_kernel(x_ref, g_ref, sc_ref, sh_ref, *refs, n_tok, t_scales):
    n_fm = len(t_scales)
    w_refs, wt_refs = refs[:n_tok], refs[n_tok:n_tok + n_fm]
    wm_ref, wmt_ref = refs[n_tok + n_fm:n_tok + n_fm + 2]
    out_refs = refs[n_tok + n_fm + 2:]
    h = _rms_mod(x_ref[0], g_ref[...], sc_ref[0], sh_ref[0])
    hb = h.astype(BF16)
    ht = h.T
    htb = ht.astype(BF16)
    h_lo = (h - hb.astype(F32)).astype(BF16)
    ht_lo = (ht - htb.astype(F32)).astype(BF16)
    pt = jnp.dot(wmt_ref[...], htb, preferred_element_type=F32)
    out_refs[-1][0] = (pt[:MISC_W] + pt[MISC_W:]
                       + jnp.dot(wmt_ref[:MISC_W, :], ht_lo, preferred_element_type=F32))
    pm = jnp.dot(hb, wm_ref[...], preferred_element_type=F32)
    out_refs[-2][0] = (pm[:, :MISC_W] + pm[:, MISC_W:]
                       + jnp.dot(h_lo, wm_ref[:, :MISC_W], preferred_element_type=F32))
    for o_ref, w_ref in zip(out_refs[:n_tok], w_refs):
        o_ref[0] = jnp.dot(hb, w_ref[...], preferred_element_type=F32).astype(BF16)
    for o_ref, wt_ref, scale in zip(out_refs[n_tok:-2], wt_refs, t_scales):
        hgt = wt_ref.shape[0]
        val = jnp.dot(wt_ref[...], htb, preferred_element_type=F32)
        val = (val if scale == 1.0 else val * scale).astype(BF16)
        if len(o_ref.shape) == 4:
            o_ref[0, 0] = val
        elif o_ref.shape[1] == hgt:
            o_ref[0] = val
        else:
            for j in range(hgt // IDX_DIM):
                o_ref[0, j * LANES:j * LANES + IDX_DIM, :] = val[j * IDX_DIM:(j + 1) * IDX_DIM, :]
                o_ref[0, j * LANES + IDX_DIM:(j + 1) * LANES, :] = jnp.zeros(
                    (LANES - IDX_DIM, val.shape[1]), BF16)


def _inproj(x, g, sc, sh, w_tok, w_fm, w_misc, t_out_heights, t_scales, tm):
    B, T, D = x.shape
    row = lambda b, i: (b, i, 0)
    col = lambda b, i: (b, 0, i)
    per_b = lambda b, i: (b, 0, 0)
    const = lambda b, i: (0, 0)
    resident = functools.partial(pl.BlockSpec, index_map=const, pipeline_mode=pl.Buffered(1))
    weights = list(w_tok) + list(w_fm) + [w_misc, w_misc.T]
    out_shape = [jax.ShapeDtypeStruct((B, T, w.shape[1]), BF16) for w in w_tok]
    out_specs = [pl.BlockSpec((1, tm, w.shape[1]), row) for w in w_tok]
    out_shape += [jax.ShapeDtypeStruct((B, hgt, T), BF16) for hgt in t_out_heights[:-1]]
    out_specs += [pl.BlockSpec((1, hgt, tm), col) for hgt in t_out_heights[:-1]]
    out_shape += [jax.ShapeDtypeStruct((B, T // tm, t_out_heights[-1], tm), BF16),
                  jax.ShapeDtypeStruct((B, T, MISC_W), F32),
                  jax.ShapeDtypeStruct((B, MISC_W, T), F32)]
    out_specs += [pl.BlockSpec((1, 1, t_out_heights[-1], tm), lambda b, i: (b, i, 0, 0)),
                  pl.BlockSpec((1, tm, MISC_W), row),
                  pl.BlockSpec((1, MISC_W, tm), col)]
    return pl.pallas_call(
        functools.partial(_inproj_kernel, n_tok=len(w_tok), t_scales=t_scales),
        out_shape=out_shape,
        grid=(B, T // tm),
        in_specs=[pl.BlockSpec((1, tm, D), row),
                  pl.BlockSpec((1, D), const),
                  pl.BlockSpec((1, 1, D), per_b),
                  pl.BlockSpec((1, 1, D), per_b)] + [resident(w.shape) for w in weights],
        out_specs=out_specs,
        compiler_params=_params(2),
        name="inproj",
    )(x, g, sc, sh, *weights)


def _split_bf16(w):
    hi = w.astype(BF16)
    return jnp.concatenate([hi, (w - hi.astype(F32)).astype(BF16)], axis=-1)


def _dsa_kernel(q_ref, k_ref, vt_ref, qi_ref, mall_ref, mt_ref, o_ref, sc_ref, lg0_ref, lg1_ref,
                acc_ref, *, T, DQ, KT, KC, topk, dh):
    i = pl.program_id(1)
    lg_refs = (lg0_ref, lg1_ref)
    nt = ((i + 1) * DQ + KT - 1) // KT
    qpos = i * DQ + lax.broadcasted_iota(I32, (1, DQ), 1)
    lim = ((qpos >> _log2(CHUNK)) + 1) << _log2(CHUNK)
    krow = lax.broadcasted_iota(I32, (KT, DQ), 0)

    def tile(kt):
        return pl.ds(pl.multiple_of(kt * KT, KT), KT)

    def half_tile(kt):
        return pl.ds(pl.multiple_of(kt * (KT // 2), KT // 2), KT // 2)

    def trunc_bf16(x):
        bits = lax.bitcast_convert_type(x, I32) & jnp.int32(-65536)
        return lax.bitcast_convert_type(bits, F32).astype(BF16)

    wi_t = mt_ref[0, WI_OFF:WI_OFF + IDX_HEADS, :] * ((IDX_HEADS ** -0.5) * (IDX_DIM ** -0.5))

    def score_tile(kt, carry):
        ki_t = mall_ref[0, tile(kt), :].astype(BF16)
        s = jnp.zeros((KT, DQ), F32)
        for h in range(IDX_HEADS):
            x = jnp.dot(ki_t, qi_ref[0, h * LANES:(h + 1) * LANES, :], preferred_element_type=F32)
            s = s + jnp.maximum(x, 0.0) * wi_t[h:h + 1, :]
        s = jnp.where(krow + kt * KT < lim, s, NEG_SCORE)
        sc_ref[tile(kt), :] = s
        lg1_ref[half_tile(kt), :] = pltpu.bitcast(trunc_bf16(s), F32)
        return carry

    lax.fori_loop(0, nt, score_tile, 0)

    def count(pred):
        def body(kt, acc):
            m = jnp.where(pred(sc_ref[tile(kt), :], kt * KT), 1.0, 0.0)
            return acc + jnp.sum(m, axis=0, keepdims=True)
        return lax.fori_loop(0, nt, body, jnp.zeros((1, DQ), F32))

    int_min = jnp.int32(-2 ** 31)

    def key_to_float(u):
        bits = jnp.where(u < 0, u ^ int_min, ~u)
        return lax.bitcast_convert_type(bits, F32)

    one_b, zero_b = jnp.ones((), BF16), jnp.zeros((), BF16)

    def count_hi(tb):
        def body(kt, acc):
            hi = pltpu.bitcast(lg1_ref[half_tile(kt), :], BF16)
            m = jnp.where(hi >= tb, one_b, zero_b)
            for j in range(KT // BF16_ROWS):
                acc = acc + m[j * BF16_ROWS:(j + 1) * BF16_ROWS, :]
            return acc
        acc = lax.fori_loop(0, nt, body, jnp.zeros((BF16_ROWS, DQ), BF16))
        return jnp.sum(acc.astype(F32), axis=0, keepdims=True)

    def hi_body(it, carry):
        u, c_ge = carry
        trial = u | (jnp.int32(1) << (KEY_BITS - 1 - it))
        cnt = count_hi(trunc_bf16(key_to_float(trial)))
        ok = cnt >= topk
        return jnp.where(ok, trial, u), jnp.where(ok, cnt, c_ge)

    def lo_body(it, carry):
        u, c_ge = carry
        trial = u | (jnp.int32(1) << (KEY_BITS // 2 - 1 - it))
        t = key_to_float(trial)
        cnt = count(lambda s, c0: s >= t)
        ok = cnt >= topk
        return jnp.where(ok, trial, u), jnp.where(ok, cnt, c_ge)

    carry = (jnp.zeros((1, DQ), I32), jnp.zeros((1, DQ), F32))
    carry = lax.fori_loop(0, KEY_BITS // 2, hi_body, carry)
    u, c_ge = lax.fori_loop(0, KEY_BITS // 2, lo_body, carry)
    thr = key_to_float(u)

    c_gt = count(lambda s, c0: s > thr)
    need = topk - c_gt
    excess = jnp.logical_and(c_ge - c_gt > need, thr > NEG_SCORE)
    has_excess = jnp.max(jnp.where(excess, 1.0, 0.0)) > 0.0
    few = thr <= NEG_SCORE
    need_ties = jnp.where(few, 0.0, need)
    thr_ge = jnp.where(few, NEG_SCORE_NEXT, thr)

    def bias_no_ties():
        def body(kt, carry):
            sc_ref[tile(kt), :] = jnp.where(sc_ref[tile(kt), :] >= thr_ge, 0.0, NEG_BIAS)
            return carry
        lax.fori_loop(0, nt, body, 0)

    def bias_with_ties():
        r = lax.broadcasted_iota(I32, (KT, KT), 0)
        c = lax.broadcasted_iota(I32, (KT, KT), 1)
        before = jnp.where(c < r, 1.0, 0.0).astype(BF16)

        def count_ties(kt, carry):
            eq = jnp.where(sc_ref[tile(kt), :] == thr, 1.0, 0.0)
            acc_ref[pl.ds(kt, 1), :] = jnp.sum(eq, axis=0, keepdims=True)
            return carry
        lax.fori_loop(0, nt, count_ties, 0)

        def body(kt, seen):
            here = acc_ref[pl.ds(kt, 1), :]
            all_kept = seen + here <= need_ties
            none_kept = seen >= need_ties
            crossing = jnp.logical_not(jnp.logical_or(all_kept, none_kept))

            def exact():
                s = sc_ref[tile(kt), :]
                eq = jnp.where(s == thr, 1.0, 0.0)
                rank = seen + jnp.dot(before, eq.astype(BF16), preferred_element_type=F32)
                keep = jnp.where(s > thr, 1.0, jnp.where(rank < need_ties, eq, 0.0))
                sc_ref[tile(kt), :] = jnp.where(keep > 0.5, 0.0, NEG_BIAS)

            def uniform():
                s = sc_ref[tile(kt), :]
                tie_bias = jnp.where(all_kept, 0.0, NEG_BIAS)
                sc_ref[tile(kt), :] = jnp.where(s > thr, 0.0,
                                                jnp.where(s == thr, tie_bias, NEG_BIAS))

            lax.cond(jnp.max(jnp.where(crossing, 1.0, 0.0)) > 0.0, exact, uniform)
            return seen + here
        lax.fori_loop(0, nt, body, jnp.zeros((1, DQ), F32))

    lax.cond(has_excess, bias_with_ties, bias_no_ties)

    def fused(h_b, h_a, kts, mx_b, den, mx_a):
        if h_a is not None:
            hs_a = slice(h_a * dh, (h_a + 1) * dh)
            lgs = [jnp.dot(k_ref[0, tile(kt), hs_a], q_ref[0, hs_a, :], preferred_element_type=F32)
                   for kt in kts]
        if h_b is not None:
            hs_b = slice(h_b * dh, (h_b + 1) * dh)
            ps = [jnp.exp2(lg_refs[h_b % 2][tile(kt), :] - mx_b) for kt in kts]
            pv = jnp.zeros((dh, DQ), F32)
            for kt, p in zip(kts, ps):
                pb = p.astype(BF16)
                for j in range(KT // KC):
                    pv = pv + jnp.dot(vt_ref[0, kt * (KT // KC) + j, hs_b, :],
                                      pb[j * KC:(j + 1) * KC, :], preferred_element_type=F32)
                den = den + jnp.sum(p, axis=0, keepdims=True)
            acc_ref[hs_b, :] += pv
        if h_a is not None:
            for kt, lg in zip(kts, lgs):
                s = lg + sc_ref[tile(kt), :]
                lg_refs[h_a % 2][tile(kt), :] = s
                mx_a = jnp.maximum(mx_a, jnp.max(s, axis=0, keepdims=True))
        return den, mx_a

    def tile_loop(h_b, h_a, mx_b):
        def run(first, count, carry):
            for i in range(count):
                carry = fused(h_b, h_a, [first + i], mx_b, *carry)
            return carry

        carry = lax.fori_loop(0, nt // ATT_UNROLL,
                              lambda j, c: run(ATT_UNROLL * j, ATT_UNROLL, c), (den0, mx0))
        done = (nt // ATT_UNROLL) * ATT_UNROLL
        width = ATT_UNROLL // 2
        while width >= 1:
            take = (nt & width) != 0
            carry = lax.cond(take, functools.partial(run, done, width), lambda c: c, carry)
            done = done + jnp.where(take, width, 0)
            width //= 2
        return carry

    mx0 = jnp.full((1, DQ), NEG_BIAS, F32)
    den0 = jnp.zeros((1, DQ), F32)
    acc_ref[...] = jnp.zeros(acc_ref.shape, F32)
    _, mx = tile_loop(None, 0, None)
    for h in range(DSA_HEADS):
        den, mx_next = tile_loop(h, h + 1 if h + 1 < DSA_HEADS else None, mx)
        hs = slice(h * dh, (h + 1) * dh)
        o_ref[0, :, hs] = (acc_ref[hs, :] / den).T.astype(BF16)
        mx = mx_next


def _dsa(qa_t, ka, va_t, qi_t, misc, misc_t, dq):
    B, T, D = ka.shape
    dh = D // DSA_HEADS
    KC = va_t.shape[3]
    KT = min(KEY_TILE, T)
    topk = min(IDX_TOPK_MAX, T // 4)
    assert KT >= topk and KT % KC == 0 and dq % CHUNK == 0
    assert T // BF16_ROWS <= 256, "packed bf16 counters must stay exact"
    blk = lambda b, i: (b, i, 0)
    whole = functools.partial(pl.BlockSpec, pipeline_mode=pl.Buffered(1))
    return pl.pallas_call(
        functools.partial(_dsa_kernel, T=T, DQ=dq, KT=KT, KC=KC, topk=topk, dh=dh),
        out_shape=jax.ShapeDtypeStruct((B, T, D), BF16),
        grid=(B, T // dq),
        in_specs=[pl.BlockSpec((1, D, dq), lambda b, i: (b, 0, i)),
                  whole((1, T, D), lambda b, i: (b, 0, 0)),
                  whole((1, T // KC, D, KC), lambda b, i: (b, 0, 0, 0)),
                  pl.BlockSpec((1, IDX_HEADS * LANES, dq), lambda b, i: (b, 0, i)),
                  whole((1, T, MISC_W), lambda b, i: (b, 0, 0)),
                  pl.BlockSpec((1, MISC_W, dq), lambda b, i: (b, 0, i))],
        out_specs=pl.BlockSpec((1, dq, D), blk),
        scratch_shapes=[pltpu.VMEM((T, dq), F32),
                        pltpu.VMEM((T, dq), F32),
                        pltpu.VMEM((T, dq), F32),
                        pltpu.VMEM((D, dq), F32)],
        compiler_params=_params(2),
        name="dsa",
    )(qa_t, ka, va_t, qi_t, misc, misc_t)


def _gla_kernel(q_ref, k_ref, v_ref, gb_ref, misc_ref, wg_ref, bg_ref, gn_ref, o_ref, st_ref,
                *, tb, dk, dv):
    @pl.when(pl.program_id(1) == 0)
    def _():
        st_ref[...] = jnp.zeros(st_ref.shape, F32)

    hk = wg_ref.shape[1] // 2
    nbb = q_ref.shape[0]
    las = []
    for bb in range(nbb):
        m = misc_ref[bb]
        m_hi = m.astype(BF16)
        m_lo = (m - m_hi.astype(F32)).astype(BF16)
        px = jnp.dot(m_hi, wg_ref[...], preferred_element_type=F32)
        x = (px[:, :hk] + px[:, hk:]
             + jnp.dot(m_lo, wg_ref[:, :hk], preferred_element_type=F32)) + bg_ref[...]
        log_a = (jnp.minimum(x, 0.0) - jnp.log1p(jnp.exp(-jnp.abs(x)))) / GATE_TAU
        la1 = log_a.astype(BF16)
        res = log_a - la1.astype(F32)
        la2 = res.astype(BF16)
        las.append((la1, la2, (res - la2.astype(F32)).astype(BF16)))

    r = lax.broadcasted_iota(I32, (CHUNK, CHUNK), 0)
    c = lax.broadcasted_iota(I32, (CHUNK, CHUNK), 1)
    causal = r >= c
    tril = jnp.where(causal, 1.0, 0.0).astype(BF16)
    gn = gn_ref[...]
    contract_last = (((1,), (1,)), ((), ()))
    contract_first = (((0,), (0,)), ((), ()))

    units = [(bb, h) for bb in range(nbb) for h in range(GLA_HEADS)]
    for ci in range(tb // CHUNK):
        rs = slice(ci * CHUNK, (ci + 1) * CHUNK)
        g_all = [sum(jnp.dot(tril, la[rs], preferred_element_type=F32) for la in las[bb])
                 for bb in range(nbb)]
        qe, kd, kdl, eg_last = {}, {}, {}, {}
        for u in units:
            bb, h = u
            ks = slice(h * dk, (h + 1) * dk)
            g = g_all[bb][:, ks]
            g_last = g[CHUNK - 1:CHUNK, :]
            kh = k_ref[bb, rs, ks].astype(F32)
            qe[u] = ((q_ref[bb, rs, ks].astype(F32) * (dk ** -0.5)) * jnp.exp(g)).astype(BF16)
            kd[u] = (kh * jnp.exp(-g)).astype(BF16)
            kdl[u] = (kh * jnp.exp(g_last - g)).astype(BF16)
            eg_last[u] = jnp.exp(g_last)
        att = {}
        for u in units:
            a = lax.dot_general(qe[u], kd[u], contract_last, preferred_element_type=F32)
            att[u] = jnp.where(causal, a, 0.0).astype(BF16)
        outs = {}
        for u in units:
            bb, h = u
            vh = v_ref[bb, rs, h * dv:(h + 1) * dv]
            st = st_ref[bb, h]
            outs[u] = (jnp.dot(att[u], vh, preferred_element_type=F32)
                       + lax.dot_general(qe[u], st.astype(BF16), contract_last,
                                         preferred_element_type=F32))
            upd = lax.dot_general(vh, kdl[u], contract_first, preferred_element_type=F32)
            st_ref[bb, h] = st * eg_last[u] + upd
        for u in units:
            bb, h = u
            vs = slice(h * dv, (h + 1) * dv)
            o = outs[u]
            y = o * lax.rsqrt(jnp.mean(o * o, axis=-1, keepdims=True) + EPS) * gn
            gate = gb_ref[bb, rs, vs].astype(F32)
            o_ref[bb, rs, vs] = (y * (gate * jax.nn.sigmoid(gate))).astype(BF16)


def _gla(qb, kb, vb, gb, misc, wg_pad, b_gate, gla_norm_g, tb):
    B, T, HK = qb.shape
    HV = vb.shape[2]
    dk, dv = HK // GLA_HEADS, HV // GLA_HEADS
    nbb = 2 if B % 2 == 0 else 1
    blk = lambda b, i: (b, i, 0)
    const = lambda b, i: (0, 0)
    return pl.pallas_call(
        functools.partial(_gla_kernel, tb=tb, dk=dk, dv=dv),
        out_shape=jax.ShapeDtypeStruct((B, T, HV), BF16),
        grid=(B // nbb, T // tb),
        in_specs=[pl.BlockSpec((nbb, tb, HK), blk),
                  pl.BlockSpec((nbb, tb, HK), blk),
                  pl.BlockSpec((nbb, tb, HV), blk),
                  pl.BlockSpec((nbb, tb, HV), blk),
                  pl.BlockSpec((nbb, tb, MISC_W), blk),
                  pl.BlockSpec((MISC_W, 2 * HK), const),
                  pl.BlockSpec((1, HK), const),
                  pl.BlockSpec((1, dv), const)],
        out_specs=pl.BlockSpec((nbb, tb, HV), blk),
        scratch_shapes=[pltpu.VMEM((nbb, GLA_HEADS, dv, dk), F32)],
        compiler_params=_params(2),
        name="gla",
    )(qb, kb, vb, gb, misc, wg_pad, b_gate.reshape(1, -1), gla_norm_g.reshape(1, -1))


def _pack_bf16_pairs(x):
    half = x.shape[1] // 2
    bits = lax.bitcast_convert_type(x.astype(BF16).astype(F32), I32)
    return (bits[:, half:] & jnp.int32(-65536)) | lax.shift_right_logical(bits[:, :half], 16)


def _unpack_bf16_pairs(w):
    lo = lax.bitcast_convert_type(w << 16, F32)
    hi = lax.bitcast_convert_type(w & jnp.int32(-65536), F32)
    return jnp.concatenate([lo, hi], axis=1)


def _merge_kernel(ya_ref, yb_ref, ga_ref, gbm_ref, x_ref, gt_ref, sc_ref, sh_ref, g2_ref,
                  wa_ref, wb_ref, wo_ref, rw_ref, rb_ref,
                  x1_ref, h2_ref, e_ref, w_ref, cnt_ref, *, n_exp):
    pa = jnp.dot(ya_ref[0], wa_ref[...], preferred_element_type=F32)
    pb = jnp.dot(yb_ref[0], wb_ref[...], preferred_element_type=F32)
    merged = (jax.nn.sigmoid(ga_ref[0].astype(F32)) * pa
              + jax.nn.sigmoid(gbm_ref[0].astype(F32)) * pb)
    mo = jnp.dot(merged.astype(BF16), wo_ref[...], preferred_element_type=F32)
    x1 = x_ref[0] + gt_ref[0] * mo
    x1_ref[0] = x1
    h2 = _rms_mod(x1, g2_ref[...], sc_ref[0], sh_ref[0])
    h2_ref[0] = _pack_bf16_pairs(h2)

    h_hi = h2.astype(BF16)
    h_lo = (h2 - h_hi.astype(F32)).astype(BF16)
    part = jnp.dot(h_hi, rw_ref[...], preferred_element_type=F32)
    logits = (part[:, :LANES] + part[:, LANES:]
              + jnp.dot(h_lo, rw_ref[:, :LANES], preferred_element_type=F32)) + rb_ref[...]
    tm = logits.shape[0]
    lane = lax.broadcasted_iota(I32, (tm, LANES), 1)
    logits = jnp.where(lane < n_exp, logits, -jnp.inf)
    top_v, top_e = [], []
    for _ in range(TOP_K):
        mx = jnp.max(logits, axis=1, keepdims=True)
        idx = jnp.min(jnp.where(logits == mx, lane, LANES), axis=1, keepdims=True)
        logits = jnp.where(lane == idx, -jnp.inf, logits)
        top_v.append(mx)
        top_e.append(idx)
    ex = [jnp.exp(v - top_v[0]) for v in top_v]
    den = ex[0]
    for t in ex[1:]:
        den = den + t
    e_out = jnp.zeros((tm, LANES), I32)
    w_out = jnp.zeros((tm, LANES), F32)
    hot = jnp.zeros((tm, LANES), F32)
    for k in range(TOP_K):
        e_out = jnp.where(lane == k, top_e[k], e_out)
        w_out = jnp.where(lane == k, ex[k] / den, w_out)
        hot = hot + jnp.where(lane == top_e[k], 1.0, 0.0)
    e_ref[0] = e_out
    w_ref[0] = w_out

    @pl.when(jnp.logical_and(pl.program_id(0) == 0, pl.program_id(1) == 0))
    def _():
        cnt_ref[...] = jnp.zeros(cnt_ref.shape, F32)

    cnt_ref[0:1, :] += jnp.sum(hot, axis=0, keepdims=True)


def _merge(ya, yb, ga, gbm, x, gt1, sc2, sh2, g2, wa, wb, wo, rw_pad, rb_pad, n_exp, tm):
    B, T, D = x.shape
    row = lambda b, i: (b, i, 0)
    per_b = lambda b, i: (b, 0, 0)
    const = lambda b, i: (0, 0)
    resident = functools.partial(pl.BlockSpec, index_map=const, pipeline_mode=pl.Buffered(1))
    return pl.pallas_call(
        functools.partial(_merge_kernel, n_exp=n_exp),
        out_shape=[jax.ShapeDtypeStruct((B, T, D), F32),
                   jax.ShapeDtypeStruct((B, T, D // 2), I32),
                   jax.ShapeDtypeStruct((B, T, LANES), I32),
                   jax.ShapeDtypeStruct((B, T, LANES), F32),
                   jax.ShapeDtypeStruct((8, LANES), F32)],
        grid=(B, T // tm),
        in_specs=[pl.BlockSpec((1, tm, D), row),
                  pl.BlockSpec((1, tm, D), row),
                  pl.BlockSpec((1, tm, D), row),
                  pl.BlockSpec((1, tm, D), row),
                  pl.BlockSpec((1, tm, D), row),
                  pl.BlockSpec((1, 1, D), per_b),
                  pl.BlockSpec((1, 1, D), per_b),
                  pl.BlockSpec((1, 1, D), per_b),
                  pl.BlockSpec((1, D), const),
                  resident((D, D)), resident((D, D)), resident((D, D)),
                  resident((D, 2 * LANES)), pl.BlockSpec((1, LANES), const)],
        out_specs=[pl.BlockSpec((1, tm, D), row),
                   pl.BlockSpec((1, tm, D // 2), row),
                   pl.BlockSpec((1, tm, LANES), row),
                   pl.BlockSpec((1, tm, LANES), row),
                   pl.BlockSpec((8, LANES), const)],
        compiler_params=_params(2),
        name="merge",
    )(ya, yb, ga, gbm, x, gt1, sc2, sh2, g2, wa, wb, wo, rw_pad, rb_pad)


def _plan_kernel(e_ref, cnt_ref, slot_ref, blk_ref, nblk_ref, base_ref, *, n_exp, tmb, nbp):
    tt = e_ref.shape[0]
    lane1 = lax.broadcasted_iota(I32, (1, LANES), 1)

    @pl.when(pl.program_id(0) == 0)
    def _():
        cnt = cnt_ref[0:1, :]
        padded = jnp.where(lane1 < n_exp, jnp.ceil(cnt / tmb) * tmb, 0.0)
        r = lax.broadcasted_iota(I32, (LANES, LANES), 0)
        c = lax.broadcasted_iota(I32, (LANES, LANES), 1)
        upper = jnp.where(r < c, 1.0, 0.0)
        start = jnp.dot(jnp.broadcast_to(padded, (8, LANES)), upper, precision=HI,
                        preferred_element_type=F32)[0:1, :]
        base_ref[...] = start
        end = start + padded
        jrow = lax.broadcasted_iota(I32, (nbp, LANES), 0).astype(F32) * tmb
        owner = jnp.sum(jnp.where(jnp.logical_and(end <= jrow, lane1 < n_exp), 1.0, 0.0),
                        axis=1, keepdims=True)
        total = jnp.sum(padded, axis=1, keepdims=True)
        last = jnp.sum(jnp.where(jnp.logical_and(end == jrow + tmb, padded > 0.0), 1.0, 0.0),
                       axis=1, keepdims=True)
        sparse_blk = jnp.where(jrow >= total, 1.0, last)
        lane_n = lax.broadcasted_iota(I32, (nbp, LANES), 1)
        blk_ref[...] = jnp.where(lane_n == 1, sparse_blk,
                                 jnp.minimum(owner, n_exp - 1.0)).astype(I32)
        nblk_ref[...] = jnp.broadcast_to(total / tmb, (8, LANES)).astype(I32)

    e = e_ref[...]
    lane = lax.broadcasted_iota(I32, (tt, LANES), 1)
    hot = jnp.zeros((tt, LANES), F32)
    for k in range(TOP_K):
        hot = hot + jnp.where(lane == e[:, k:k + 1], 1.0, 0.0)
    r = lax.broadcasted_iota(I32, (tt, tt), 0)
    c = lax.broadcasted_iota(I32, (tt, tt), 1)
    lower = jnp.where(r > c, 1.0, 0.0).astype(BF16)
    rank = jnp.dot(lower, hot.astype(BF16), preferred_element_type=F32)
    pos = base_ref[...] + rank
    out = jnp.zeros((tt, LANES), F32)
    for k in range(TOP_K):
        sk = jnp.sum(jnp.where(lane == e[:, k:k + 1], pos, 0.0), axis=1, keepdims=True)
        out = jnp.where(lane == k, sk, out)
    slot_ref[...] = out.astype(I32)
    base_ref[...] += jnp.sum(hot, axis=0, keepdims=True)


def _plan(e_pad, cnt, n_exp, tmb, nbp, tt):
    N = e_pad.shape[0]
    return pl.pallas_call(
        functools.partial(_plan_kernel, n_exp=n_exp, tmb=tmb, nbp=nbp),
        out_shape=[jax.ShapeDtypeStruct((N, LANES), I32),
                   jax.ShapeDtypeStruct((nbp, LANES), I32),
                   jax.ShapeDtypeStruct((8, LANES), I32)],
        grid=(N // tt,),
        in_specs=[pl.BlockSpec((tt, LANES), lambda i: (i, 0)),
                  pl.BlockSpec((8, LANES), lambda i: (0, 0))],
        out_specs=[pl.BlockSpec((tt, LANES), lambda i: (i, 0)),
                   pl.BlockSpec((nbp, LANES), lambda i: (0, 0)),
                   pl.BlockSpec((8, LANES), lambda i: (0, 0))],
        scratch_shapes=[pltpu.VMEM((1, LANES), F32)],
        compiler_params=_params(1),
        name="moe_plan",
    )(e_pad, cnt)


def _dispatch_kernel(slot_ref, sparse_ref, h_ref, xs_ref, zeros, sem, zsem, *, tt, tmb):
    @pl.when(pl.program_id(0) == 0)
    def _():
        zeros[...] = jnp.zeros(zeros.shape, zeros.dtype)

        def clear(j):
            return pltpu.make_async_copy(
                zeros, xs_ref.at[pl.ds(pl.multiple_of(j * tmb, tmb), tmb)], zsem)

        def start_clear(j, carry):
            @pl.when(sparse_ref[j] != 0)
            def _():
                clear(j).start()
            return carry

        def wait_clear(j, carry):
            @pl.when(sparse_ref[j] != 0)
            def _():
                clear(j).wait()
            return carry

        lax.fori_loop(0, sparse_ref.shape[0], start_clear, 0)
        lax.fori_loop(0, sparse_ref.shape[0], wait_clear, 0)

    def start(g, carry):
        t0 = pl.multiple_of(g * ROW_GROUP, ROW_GROUP)
        for j in range(ROW_GROUP):
            for k in range(TOP_K):
                s = slot_ref[(t0 + j) * TOP_K + k]
                pltpu.make_async_copy(h_ref.at[pl.ds(t0 + j, 1)], xs_ref.at[pl.ds(s, 1)],
                                      sem).start(priority=k % 2)
        return carry

    lax.fori_loop(0, tt // ROW_GROUP, start, 0)
    for k in range(TOP_K):
        pltpu.make_async_copy(h_ref, xs_ref.at[pl.ds(0, tt)], sem).wait()


def _dispatch(slots_flat, sparse_blk, h2, tt, tmb):
    N, D = h2.shape
    nb = sparse_blk.shape[0]
    return pl.pallas_call(
        functools.partial(_dispatch_kernel, tt=tt, tmb=tmb),
        out_shape=jax.ShapeDtypeStruct((nb * tmb, D), h2.dtype),
        grid=(N // tt,),
        in_specs=[pl.BlockSpec((tt * TOP_K,), lambda i: (i,), memory_space=pltpu.SMEM),
                  pl.BlockSpec((nb,), lambda i: (0,), memory_space=pltpu.SMEM),
                  pl.BlockSpec((tt, D), lambda i: (i, 0))],
        out_specs=pl.BlockSpec(memory_space=pl.ANY),
        scratch_shapes=[pltpu.VMEM((tmb, D), h2.dtype), pltpu.SemaphoreType.DMA,
                        pltpu.SemaphoreType.DMA],
        compiler_params=_params(1),
        name="moe_dispatch",
    )(slots_flat, sparse_blk, h2)


def _experts_kernel(blk_ref, nblk_ref, xs_ref, wgu_ref, bgu_ref, wd_ref, bd_ref, ys_ref,
                    wgu_bf, wd_bf, *, ff):
    j = pl.program_id(0)
    e = blk_ref[j]
    fresh = jnp.logical_or(j == 0, e != blk_ref[jnp.maximum(j - 1, 0)])
    active = j < nblk_ref[0]

    @pl.when(jnp.logical_and(active, fresh))
    def _():
        wgu_bf[...] = wgu_ref[0].astype(BF16)
        wd_bf[...] = wd_ref[0].astype(BF16)

    @pl.when(active)
    def _():
        x = _unpack_bf16_pairs(xs_ref[...]).astype(BF16)
        hgu = jnp.dot(x, wgu_bf[...], preferred_element_type=F32) + bgu_ref[0]
        gate = jnp.minimum(hgu[:, :ff], SWIGLU_LIMIT)
        up = jnp.clip(hgu[:, ff:], -SWIGLU_LIMIT, SWIGLU_LIMIT)
        act = (up + 1.0) * (gate * jax.nn.sigmoid(SWIGLU_ALPHA * gate))
        y = jnp.dot(act.astype(BF16), wd_bf[...], preferred_element_type=F32) + bd_ref[0]
        ys_ref[...] = _pack_bf16_pairs(y)

    @pl.when(jnp.logical_not(active))
    def _():
        ys_ref[...] = jnp.zeros(ys_ref.shape, I32)


def _experts(blk_e, nblk, xs, w_gate_up, b_gate_up, w_down, b_down, tmb):
    P, dp = xs.shape
    E, D, ff2 = w_gate_up.shape
    ff = ff2 // 2
    by_expert = lambda j, blk, nb: (blk[j], 0, 0)
    return pl.pallas_call(
        functools.partial(_experts_kernel, ff=ff),
        out_shape=jax.ShapeDtypeStruct((P, dp), I32),
        grid_spec=pltpu.PrefetchScalarGridSpec(
            num_scalar_prefetch=2,
            grid=(P // tmb,),
            in_specs=[pl.BlockSpec((tmb, dp), lambda j, blk, nb: (j, 0)),
                      pl.BlockSpec((1, D, ff2), by_expert),
                      pl.BlockSpec((1, 1, ff2), by_expert),
                      pl.BlockSpec((1, ff, D), by_expert),
                      pl.BlockSpec((1, 1, D), by_expert)],
            out_specs=pl.BlockSpec((tmb, dp), lambda j, blk, nb: (j, 0)),
            scratch_shapes=[pltpu.VMEM((D, ff2), BF16), pltpu.VMEM((ff, D), BF16)]),
        compiler_params=_params(1),
        name="moe_experts",
    )(blk_e, nblk, xs, w_gate_up, b_gate_up.reshape(E, 1, ff2), w_down, b_down.reshape(E, 1, D))


def _combine_kernel(slot_ref, ys_ref, w_ref, x1_ref, gt_ref, fg_ref, o_ref, buf, sem, *, tt, nt):
    step = pl.program_id(0) * nt + pl.program_id(1)
    nsteps = pl.num_programs(0) * nt

    def issue(s):
        b = s & 1

        def body(g, carry):
            t0 = pl.multiple_of(g * ROW_GROUP, ROW_GROUP)
            for j in range(ROW_GROUP):
                for k in range(TOP_K):
                    idx = slot_ref[(s * tt + t0 + j) * TOP_K + k]
                    pltpu.make_async_copy(ys_ref.at[pl.ds(idx, 1)], buf.at[b, k, pl.ds(t0 + j, 1)],
                                          sem.at[b]).start(priority=k % 2)
            return carry
        lax.fori_loop(0, tt // ROW_GROUP, body, 0)

    @pl.when(step == 0)
    def _():
        issue(step)

    @pl.when(step + 1 < nsteps)
    def _():
        issue(step + 1)

    cur = step & 1
    for k in range(TOP_K):
        pltpu.make_async_copy(ys_ref.at[pl.ds(0, tt)], buf.at[cur, k], sem.at[cur]).wait()

    w = w_ref[0]
    moe = _unpack_bf16_pairs(buf[cur, 0]) * w[:, 0:1]
    for k in range(1, TOP_K):
        moe = moe + _unpack_bf16_pairs(buf[cur, k]) * w[:, k:k + 1]
    x2 = x1_ref[0] + gt_ref[0] * moe
    y = x2 * lax.rsqrt(jnp.mean(x2 * x2, axis=-1, keepdims=True) + EPS)
    o_ref[0] = y * fg_ref[...]


def _combine(slots_flat, ys, w_pad, x1, gt2, final_g, tt):
    B, T, D = x1.shape
    nt = T // tt
    row = lambda b, i, slots: (b, i, 0)
    return pl.pallas_call(
        functools.partial(_combine_kernel, tt=tt, nt=nt),
        out_shape=jax.ShapeDtypeStruct((B, T, D), F32),
        grid_spec=pltpu.PrefetchScalarGridSpec(
            num_scalar_prefetch=1,
            grid=(B, nt),
            in_specs=[pl.BlockSpec(memory_space=pl.ANY),
                      pl.BlockSpec((1, tt, LANES), row),
                      pl.BlockSpec((1, tt, D), row),
                      pl.BlockSpec((1, 1, D), lambda b, i, slots: (b, 0, 0)),
                      pl.BlockSpec((1, D), lambda b, i, slots: (0, 0))],
            out_specs=pl.BlockSpec((1, tt, D), row),
            scratch_shapes=[pltpu.VMEM((2, TOP_K, tt, ys.shape[1]), ys.dtype),
                            pltpu.SemaphoreType.DMA((2,))]),
        compiler_params=_params(2),
        name="moe_combine",
    )(slots_flat, ys, w_pad, x1, gt2, final_g.reshape(1, -1))


class _Tiles(NamedTuple):
    inproj: int
    dsa_q: int
    gla: int
    merge: int
    plan: int
    dispatch: int
    expert: int
    combine: int


def _tiles(T):
    return _Tiles(inproj=min(512, T), dsa_q=min(512, T), gla=min(512, T), merge=min(1024, T),
                  plan=min(1024, T), dispatch=min(256, T), expert=256, combine=min(256, T))


def _layer(x, c, w_ada, b_ada, norm1_g, w_in, w_gate_lr, b_gate, gla_norm_g, w_branch_a,
           w_branch_b, w_out, norm2_g, router_w, router_b, w_gate_up, b_gate_up, w_down, b_down,
           final_g):
    B, T, D = x.shape
    N = B * T
    tiles = _tiles(T)
    n_exp = router_w.shape[1]
    hk = w_gate_lr.shape[1]
    hv = w_branch_b.shape[0]
    ni = IDX_HEADS * IDX_DIM

    c8 = jnp.zeros((8, D), F32).at[:B].set(c)
    mod = _adaln(c8, w_ada, b_ada)[:B]
    sh1, sc1, gt1, sh2, sc2, gt2 = [m.reshape(B, 1, D) for m in jnp.split(mod, 6, axis=-1)]

    sizes = (D, D, D, ni, IDX_DIM, IDX_HEADS, hk, hk, hv, hv, GATE_RANK, D, D)
    offs = [0]
    for s in sizes:
        offs.append(offs[-1] + s)
    seg = lambda j: w_in[:, offs[j]:offs[j + 1]]
    w_tok = [seg(j).astype(BF16) for j in (1, 6, 7, 8, 9, 11, 12)]
    w_fm = [seg(j).T.astype(BF16) for j in (0, 3, 2)]
    t_out_heights = (sizes[0], IDX_HEADS * LANES, sizes[2])
    t_scales = ((D // DSA_HEADS) ** -0.5 * LOG2_E, 1.0, 1.0)
    w_misc = jnp.concatenate(
        [seg(4), seg(5), seg(10), jnp.zeros((D, MISC_W - IDX_DIM - IDX_HEADS - GATE_RANK), F32)],
        axis=1)
    ka, qb, kb, vb, gb, ga, gbm, qa_t, qi_t, va_t, misc, misc_t = _inproj(
        x, norm1_g.reshape(1, D), sc1, sh1, w_tok, w_fm, _split_bf16(w_misc), t_out_heights,
        t_scales, tiles.inproj)

    ya = _dsa(qa_t, ka, va_t, qi_t, misc, misc_t, tiles.dsa_q)

    wg_pad = jnp.zeros((MISC_W, hk), F32).at[LR_OFF:LR_OFF + GATE_RANK].set(w_gate_lr)
    yb = _gla(qb, kb, vb, gb, misc, _split_bf16(wg_pad), b_gate, gla_norm_g, tiles.gla)

    rw_pad = _split_bf16(jnp.zeros((D, LANES), F32).at[:, :n_exp].set(router_w))
    rb_pad = jnp.zeros((1, LANES), F32).at[0, :n_exp].set(router_b)
    x1, h2, e_pad, w_pad, cnt = _merge(
        ya, yb, ga, gbm, x, gt1, sc2, sh2, norm2_g.reshape(1, D),
        w_branch_a.astype(BF16), w_branch_b.astype(BF16), w_out.astype(BF16),
        rw_pad, rb_pad, n_exp, tiles.merge)

    tmb = tiles.expert
    nb = -(-N * TOP_K // tmb) + n_exp
    nbp = -(-nb // ROW_GROUP) * ROW_GROUP
    slots, blk_e, nblk = _plan(e_pad.reshape(N, LANES), cnt, n_exp, tmb, nbp, tiles.plan)

    slots_flat = slots[:, :TOP_K].reshape(-1)
    xs = _dispatch(slots_flat, blk_e[:nb, 1], h2.reshape(N, D // 2), tiles.dispatch, tmb)
    ys = _experts(blk_e[:nb, 0], nblk[0, :1], xs, w_gate_up, b_gate_up, w_down, b_down, tmb)
    return _combine(slots_flat, ys, w_pad, x1, gt2, final_g, tiles.combine)


def kernel(x, c, w_ada, b_ada, norm1_g, w_in, w_gate_lr, b_gate, gla_norm_g, w_branch_a, w_branch_b,
           w_out, norm2_g, router_w, router_b, w_gate_up, b_gate_up, w_down, b_down, final_g):
    assert w_ada.shape[0] == 1, "single-layer block"
    return _layer(x, c, w_ada[0], b_ada[0], norm1_g[0], w_in[0], w_gate_lr[0], b_gate[0],
                  gla_norm_g[0], w_branch_a[0], w_branch_b[0], w_out[0], norm2_g[0], router_w[0],
                  router_b[0], w_gate_up[0], b_gate_up[0], w_down[0], b_down[0], final_g)
```
